```python
import math
import jax, jax.numpy as jnp
from jax import lax
import numpy as np

D_MODEL = 4096
BATCH = 2
SEQ = 8192
DEPTH = 4

CHUNK = 64
N_MIXERS = 2
N_HEADS = 32
HEAD_DIM = D_MODEL // N_HEADS
N_KV_HEADS = 8
GROUP = N_HEADS // N_KV_HEADS
D_ATTN = N_HEADS * HEAD_DIM
N_IDX_HEADS = 32
IDX_HEAD_DIM = 64
TOPK_MAX = 256
Q_BLOCK = 128
ROPE_THETA = 10000.0
ATTN_SIZES = (D_ATTN, N_KV_HEADS * HEAD_DIM, N_KV_HEADS * HEAD_DIM, N_IDX_HEADS * IDX_HEAD_DIM, N_IDX_HEADS, IDX_HEAD_DIM)
ATTN_IN = sum(ATTN_SIZES)
ATTN_SPLIT_POINTS = tuple(int(v) for v in np.cumsum(ATTN_SIZES)[:-1])
HGRN_EXPAND = 128
HGRN_HEADS = D_MODEL // HGRN_EXPAND
HGRN_KEY_DIM = HGRN_HEADS * HGRN_EXPAND
HGRN_HEAD_V = D_MODEL // HGRN_HEADS
D_FF = 4 * D_MODEL
ALPHA = (2.0 * DEPTH) ** 0.25
BETA = (8.0 * DEPTH) ** -0.25
LN_EPS = 1e-5
RMS_EPS = 1e-6
N_ATTN_LAYERS = (DEPTH + 1) // 2
N_HGRN_LAYERS = DEPTH // 2

kernel_name = "dsa_hgrn2_deepnorm_hybrid"


def layer_norm(x, g, b):
    xf = x.astype(jnp.float32)
    mu = jnp.mean(xf, axis=-1, keepdims=True)
    var = jnp.mean(jnp.square(xf - mu), axis=-1, keepdims=True)
    y = (xf - mu) * lax.rsqrt(var + LN_EPS) * g.astype(jnp.float32) + b.astype(jnp.float32)
    return y.astype(x.dtype)


def rope_tables(seq, dim):
    inv = 1.0 / (ROPE_THETA ** (jnp.arange(0, dim, 2, dtype=jnp.float32) / dim))
    ang = jnp.arange(seq, dtype=jnp.float32)[:, None] * inv[None, :]
    return jnp.cos(ang), jnp.sin(ang)


def apply_rope(x, cos, sin):
    half = x.shape[-1] // 2
    x1 = x[..., :half].astype(jnp.float32)
    x2 = x[..., half:].astype(jnp.float32)
    return jnp.concatenate([x1 * cos - x2 * sin, x2 * cos + x1 * sin], axis=-1).astype(x.dtype)


def dsa_mixer(x, w_in, w_out, kln_g, kln_b):
    B, S, _ = x.shape
    topk = min(TOPK_MAX, S // 4)
    z = x @ w_in
    q, k, v, qi, wi, ki = jnp.split(z, ATTN_SPLIT_POINTS, axis=-1)
    cos, sin = rope_tables(S, HEAD_DIM)
    q = apply_rope(q.reshape(B, S, N_HEADS, HEAD_DIM), cos[:, None], sin[:, None])
    k = apply_rope(k.reshape(B, S, N_KV_HEADS, HEAD_DIM), cos[:, None], sin[:, None])
    v = v.reshape(B, S, N_KV_HEADS, HEAD_DIM)
    ci, si = rope_tables(S, IDX_HEAD_DIM)
    qi = apply_rope(qi.reshape(B, S, N_IDX_HEADS, IDX_HEAD_DIM), ci[:, None], si[:, None])
    ki = apply_rope(layer_norm(ki, kln_g, kln_b), ci, si)
    wi = wi.astype(jnp.float32) * (N_IDX_HEADS ** -0.5)
    key_pos = jnp.arange(S)

    def block(blk):
        start = blk * Q_BLOCK
        q_pos = start + jnp.arange(Q_BLOCK)
        chunk_end = (q_pos // CHUNK + 1) * CHUNK - 1
        qi_b = lax.dynamic_slice_in_dim(qi, start, Q_BLOCK, axis=1)
        wi_b = lax.dynamic_slice_in_dim(wi, start, Q_BLOCK, axis=1)
        q_b = lax.dynamic_slice_in_dim(q, start, Q_BLOCK, axis=1)
        logits = jnp.einsum('bthd,bsd->bths', qi_b, ki, preferred_element_type=jnp.float32) * (IDX_HEAD_DIM ** -0.5)
        score = jnp.einsum('bths,bth->bts', jax.nn.relu(logits), wi_b)
        admissible = key_pos[None, :] <= chunk_end[:, None]
        score = jnp.where(admissible[None], score, -jnp.inf)
        _, sel = lax.top_k(score, topk)
        valid = sel <= chunk_end[None, :, None]
        k_sel = jax.vmap(lambda kk, ii: kk[ii])(k, sel)
        v_sel = jax.vmap(lambda vv, ii: vv[ii])(v, sel)
        qg = q_b.reshape(B, Q_BLOCK, N_KV_HEADS, GROUP, HEAD_DIM)
        s = jnp.einsum('btgrd,btkgd->btgrk', qg, k_sel, preferred_element_type=jnp.float32) * (HEAD_DIM ** -0.5)
        s = jnp.where(valid[:, :, None, None, :], s, -jnp.inf)
        p = jax.nn.softmax(s, axis=-1).astype(v.dtype)
        o = jnp.einsum('btgrk,btkgd->btgrd', p, v_sel)
        return o.reshape(B, Q_BLOCK, D_ATTN)

    out = lax.map(block, jnp.arange(S // Q_BLOCK))
    out = jnp.moveaxis(out, 0, 1).reshape(B, S, D_ATTN)
    return out @ w_out


def hgrn2_mixer(x, w_in, lb, norm_g, w_out):
    B, S, _ = x.shape
    n_chunks = S // CHUNK
    q, f, i, g = jnp.split(x @ w_in, 4, axis=-1)
    q = jax.nn.silu(q.astype(jnp.float32))
    f = lb + (1.0 - lb) * jax.nn.sigmoid(f.astype(jnp.float32))
    log_f = jnp.log(f)
    kin = 1.0 - f
    v = i.astype(jnp.float32)

    def to_chunks(t, d):
        return t.reshape(B, n_chunks, CHUNK, HGRN_HEADS, d).transpose(1, 0, 3, 2, 4)

    qc = to_chunks(q, HGRN_EXPAND)
    kc = to_chunks(kin, HGRN_EXPAND)
    gc = to_chunks(log_f, HGRN_EXPAND)
    vc = to_chunks(v, HGRN_HEAD_V)
    causal = jnp.tril(jnp.ones((CHUNK, CHUNK), dtype=bool))

    def step(state, inp):
        qt, kt, gt, vt = inp
        bcum = jnp.cumsum(gt, axis=2)
        o_inter = jnp.einsum('bhtk,bhkv->bhtv', qt * jnp.exp(bcum), state)
        diff = bcum[:, :, :, None, :] - bcum[:, :, None, :, :]
        decay = jnp.exp(jnp.where(causal[:, :, None], diff, -jnp.inf))
        attn = jnp.einsum('bhtk,bhsk,bhtsk->bhts', qt, kt, decay)
        o = o_inter + jnp.einsum('bhts,bhsv->bhtv', attn, vt)
        b_last = bcum[:, :, -1:, :]
        state = jnp.exp(b_last[:, :, 0, :])[..., None] * state + jnp.einsum('bhsk,bhsv->bhkv', kt * jnp.exp(b_last - bcum), vt)
        return state, o

    state0 = jnp.zeros((B, HGRN_HEADS, HGRN_EXPAND, HGRN_HEAD_V), jnp.float32)
    _, o = lax.scan(step, state0, (qc, kc, gc, vc))
    o = o.transpose(1, 0, 3, 2, 4).reshape(B, S, HGRN_HEADS, HGRN_HEAD_V)
    o = o * lax.rsqrt(jnp.mean(jnp.square(o), axis=-1, keepdims=True) + RMS_EPS) * norm_g.astype(jnp.float32)
    o = (o.reshape(B, S, D_MODEL) * jax.nn.silu(g.astype(jnp.float32))).astype(x.dtype)
    return o @ w_out


def sq_relu_mlp(x, w_up, w_down):
    h = jnp.square(jax.nn.relu(x @ w_up))
    return h @ w_down


def setup_inputs(seed: int = 0) -> dict:
    key = jax.random.key(seed)
    ks = jax.random.split(key, 16)
    f32 = jnp.float32
    nrm = lambda k, shape, scale: jax.random.normal(k, shape, f32) * scale
    return {
        "x": nrm(ks[0], (BATCH, SEQ, D_MODEL), 1.0),
        "attn_w_in": nrm(ks[1], (N_ATTN_LAYERS, D_MODEL, ATTN_IN), D_MODEL ** -0.5),
        "attn_w_out": nrm(ks[2], (N_ATTN_LAYERS, D_ATTN, D_MODEL), BETA * D_ATTN ** -0.5),
        "idx_k_ln_g": 1.0 + nrm(ks[3], (N_ATTN_LAYERS, IDX_HEAD_DIM), 0.02),
        "idx_k_ln_b": nrm(ks[4], (N_ATTN_LAYERS, IDX_HEAD_DIM), 0.02),
        "hgrn_w_in": nrm(ks[5], (N_HGRN_LAYERS, D_MODEL, 4 * D_MODEL), D_MODEL ** -0.5),
        "hgrn_lower_bounds": nrm(ks[6], (DEPTH, HGRN_KEY_DIM), 0.1),
        "hgrn_norm_g": 1.0 + nrm(ks[7], (N_HGRN_LAYERS, HGRN_HEAD_V), 0.02),
        "hgrn_w_out": nrm(ks[8], (N_HGRN_LAYERS, D_MODEL, D_MODEL), BETA * D_MODEL ** -0.5),
        "mix_ln_g": 1.0 + nrm(ks[9], (DEPTH, D_MODEL), 0.02),
        "mix_ln_b": nrm(ks[10], (DEPTH, D_MODEL), 0.02),
        "mlp_w_up": nrm(ks[11], (DEPTH, D_MODEL, D_FF), D_MODEL ** -0.5),
        "mlp_w_down": nrm(ks[12], (DEPTH, D_FF, D_MODEL), BETA * D_FF ** -0.5),
        "mlp_ln_g": 1.0 + nrm(ks[13], (DEPTH, D_MODEL), 0.02),
        "mlp_ln_b": nrm(ks[14], (DEPTH, D_MODEL), 0.02),
    }


def reference(x, attn_w_in, attn_w_out, idx_k_ln_g, idx_k_ln_b, hgrn_w_in, hgrn_lower_bounds, hgrn_norm_g, hgrn_w_out, mix_ln_g, mix_ln_b, mlp_w_up, mlp_w_down, mlp_ln_g, mlp_ln_b):
    lb_all = jnp.cumsum(jax.nn.softmax(hgrn_lower_bounds.astype(jnp.float32), axis=0), axis=0)
    lb_all = lb_all - lb_all[0:1]
    for layer in range(DEPTH):
        j = layer // N_MIXERS
        if layer % N_MIXERS == 0:
            h = dsa_mixer(x, attn_w_in[j], attn_w_out[j], idx_k_ln_g[j], idx_k_ln_b[j])
        else:
            h = hgrn2_mixer(x, hgrn_w_in[j], lb_all[layer], hgrn_norm_g[j], hgrn_w_out[j])
        x = layer_norm(ALPHA * x + h, mix_ln_g[layer], mix_ln_b[layer])
        h = sq_relu_mlp(x, mlp_w_up[layer], mlp_w_down[layer])
        x = layer_norm(ALPHA * x + h, mlp_ln_g[layer], mlp_ln_b[layer])
    return x
```

```python
import functools
import math

import jax
import jax.numpy as jnp
from jax import lax
from jax.experimental import pallas as pl
from jax.experimental.pallas import tpu as pltpu

HEAD_DIM = 128
GROUP = 4
IDX_HEAD_DIM = 64
TOPK_MAX = 256
CHUNK = 64
ROPE_THETA = 10000.0
HGRN_EXPAND = 128
LN_EPS = 1e-5
RMS_EPS = 1e-6

V7X_VMEM_BYTES = 64 * 1024 * 1024
VMEM_LIMIT_BYTES = V7X_VMEM_BYTES * 7 // 8
LANES = 128

F32 = jnp.float32
BF16 = jnp.bfloat16
INT32_MIN = -(2 ** 31)


def _compiler_params(semantics):
    return pltpu.CompilerParams(dimension_semantics=semantics, vmem_limit_bytes=VMEM_LIMIT_BYTES)


def _pick(n, pref):
    t = min(n, pref)
    while n % t:
        t //= 2
    return t


def _rope128(x, cos, sin):
    return x * cos + pltpu.roll(x, 64, 1) * sin


def _rope64(x, cos, sin):
    lane = lax.broadcasted_iota(jnp.int32, x.shape, 1)
    first_half = (lane % 64) < 32
    partner = jnp.where(first_half, pltpu.roll(x, 96, 1), pltpu.roll(x, 32, 1))
    return x * cos + partner * sin


def _apply_epilogue(kind, acc, aux, j):
    tn = acc.shape[1]
    if kind == "none":
        return acc
    if kind == "relu2":
        r = jnp.maximum(acc, 0.0)
        return r * r
    if kind == "silu":
        return acc * jax.nn.sigmoid(acc)
    if kind == "fgate":
        lb = aux[0][...]
        return lb + (1.0 - lb) * jax.nn.sigmoid(acc)
    if kind in ("rope128", "rope64"):
        cos = aux[0][...]
        sin = aux[1][...]
        fn = _rope128 if kind == "rope128" else _rope64
        parts = [fn(acc[:, c * LANES:(c + 1) * LANES], cos, sin) for c in range(tn // LANES)]
        return jnp.concatenate(parts, axis=1)
    if kind == "qkv":
        cos = aux[0][...]
        sin = aux[1][...]
        n_rope = aux[2]
        parts = []
        for c in range(tn // LANES):
            x = acc[:, c * LANES:(c + 1) * LANES]
            parts.append(jnp.where(j * (tn // LANES) + c < n_rope, _rope128(x, cos, sin), x))
        return jnp.concatenate(parts, axis=1)
    if kind == "idx_tail":
        cos, sin, g, b, wscale = aux[0][...], aux[1][...], aux[2][...], aux[3][...], aux[4]
        lane = lax.broadcasted_iota(jnp.int32, acc.shape, 1)
        is_k = lane < IDX_HEAD_DIM
        mu = jnp.sum(jnp.where(is_k, acc, 0.0), axis=1, keepdims=True) * (1.0 / IDX_HEAD_DIM)
        d = jnp.where(is_k, acc - mu, 0.0)
        var = jnp.sum(d * d, axis=1, keepdims=True) * (1.0 / IDX_HEAD_DIM)
        y = d * lax.rsqrt(var + LN_EPS) * g + b
        y = _rope64(y, cos, sin)
        return jnp.where(is_k, y, acc * wscale)
    raise ValueError(kind)


def _mm_kernel(*refs, nk, kind, n_aux_refs, static_aux):
    a_ref, w_ref = refs[0], refs[1]
    aux_refs = list(refs[2:2 + n_aux_refs]) + list(static_aux)
    o_ref = refs[2 + n_aux_refs]
    j = pl.program_id(1)
    if nk == 1:
        acc = jnp.dot(a_ref[...], w_ref[...], preferred_element_type=F32)
        o_ref[...] = _apply_epilogue(kind, acc, aux_refs, j).astype(o_ref.dtype)
        return
    acc_ref = refs[3 + n_aux_refs]
    k = pl.program_id(2)

    @pl.when(k == 0)
    def _():
        acc_ref[...] = jnp.zeros_like(acc_ref)

    acc_ref[...] += jnp.dot(a_ref[...], w_ref[...], preferred_element_type=F32)

    @pl.when(k == nk - 1)
    def _():
        o_ref[...] = _apply_epilogue(kind, acc_ref[...], aux_refs, j).astype(o_ref.dtype)


def _matmul(a, w, *, kind="none", out_dtype=F32, row_aux=(), col_aux=(), static_aux=(),
            seq=None, tm=1024, tn=1024, tk=4096):
    M, K = a.shape
    N = w.shape[1]
    tm = _pick(M if seq is None else seq, tm)
    tn = _pick(N, tn)
    tk = _pick(K, tk)
    nk = K // tk
    grid = (M // tm, N // tn, nk)
    in_specs = [pl.BlockSpec((tm, tk), lambda i, j, k: (i, k)),
                pl.BlockSpec((tk, tn), lambda i, j, k: (k, j))]
    args = [a, w]
    for t in row_aux:
        nrow = seq // tm
        in_specs.append(pl.BlockSpec((tm, t.shape[1]), lambda i, j, k, nrow=nrow: (i % nrow, 0)))
        args.append(t)
    for t in col_aux:
        in_specs.append(pl.BlockSpec((1, tn), lambda i, j, k: (0, j)))
        args.append(t)
    scratch = [] if nk == 1 else [pltpu.VMEM((tm, tn), F32)]
    kern = functools.partial(_mm_kernel, nk=nk, kind=kind, n_aux_refs=len(row_aux) + len(col_aux),
                             static_aux=tuple(static_aux))
    return pl.pallas_call(
        kern,
        grid=grid,
        in_specs=in_specs,
        out_specs=pl.BlockSpec((tm, tn), lambda i, j, k: (i, j)),
        out_shape=jax.ShapeDtypeStruct((M, N), out_dtype),
        scratch_shapes=scratch,
        compiler_params=_compiler_params(("parallel", "parallel", "arbitrary")),
        name="mm_" + kind,
    )(*args)


def _ln_kernel(x_ref, h_ref, g_ref, b_ref, o_ref, ob_ref, *, alpha):
    y = alpha * x_ref[...] + h_ref[...]
    mu = jnp.mean(y, axis=-1, keepdims=True)
    d = y - mu
    var = jnp.mean(d * d, axis=-1, keepdims=True)
    out = d * lax.rsqrt(var + LN_EPS) * g_ref[...] + b_ref[...]
    o_ref[...] = out
    ob_ref[...] = out.astype(BF16)


def _residual_ln(x, h, g, b, alpha, tm=256):
    M, D = x.shape
    tm = _pick(M, tm)
    row = pl.BlockSpec((tm, D), lambda i: (i, 0))
    vec = pl.BlockSpec((1, D), lambda i: (0, 0))
    return pl.pallas_call(
        functools.partial(_ln_kernel, alpha=alpha),
        grid=(M // tm,),
        in_specs=[row, row, vec, vec],
        out_specs=[row, row],
        out_shape=[jax.ShapeDtypeStruct((M, D), F32), jax.ShapeDtypeStruct((M, D), BF16)],
        compiler_params=_compiler_params(("parallel",)),
        name="residual_ln",
    )(x, h, g.reshape(1, D), b.reshape(1, D))


def _dsa_kernel(q_ref, k_ref, v_ref, qi_ref, wi_ref, kie_ref, kio_ref, o_ref,
                key_ref, wexp_ref, m_ref, l_ref, acc_ref, *, TQ, TS, NI, topk):
    qb = pl.program_id(1)
    g = pl.program_id(2)
    t0 = qb * TQ
    n_kt = (t0 + TQ + TS - 1) // TS
    reps = TS // LANES

    t_idx = t0 + lax.broadcasted_iota(jnp.int32, (TQ, 1), 0)
    chunk_end = (t_idx // CHUNK + 1) * CHUNK - 1

    def admissible(kt):
        s_idx = kt * TS + lax.broadcasted_iota(jnp.int32, (1, TS), 1)
        return s_idx <= chunk_end

    @pl.when(g == 0)
    def _select():
        w_all = wi_ref[0] * (IDX_HEAD_DIM ** -0.5)
        for h in range(NI):
            wexp_ref[h] = jnp.broadcast_to(w_all[:, h:h + 1], (TQ, LANES))

        def score_tile(kt, carry):
            s0 = pl.multiple_of(kt * TS, TS)
            ke = kie_ref[0, pl.ds(s0, TS), :]
            ko = kio_ref[0, pl.ds(s0, TS), :]
            acc = jnp.zeros((TQ, TS), F32)
            for h in range(NI):
                slab = qi_ref[0, :, (h // 2) * LANES:(h // 2 + 1) * LANES]
                kk = ke if h % 2 == 0 else ko
                lg = lax.dot_general(slab, kk, (((1,), (1,)), ((), ())), preferred_element_type=F32)
                w = jnp.concatenate([wexp_ref[h]] * reps, axis=1)
                acc = acc + w * jnp.maximum(lg, 0.0)
            sc = jnp.where(admissible(kt), acc, -jnp.inf)
            bits = pltpu.bitcast(sc, jnp.int32)
            key_ref[kt] = bits ^ ((bits >> 31) & jnp.int32(0x7FFFFFFF))
            return carry

        lax.fori_loop(0, n_kt, score_tile, 0)

        def bit_step(i, thr):
            cand = thr ^ (jnp.int32(1) << (31 - i))
            cand_w = jnp.concatenate([cand] * reps, axis=1)

            def count_tile(kt, c):
                ge = (key_ref[kt] >= cand_w).astype(jnp.int32)
                for r in range(reps):
                    c = c + ge[:, r * LANES:(r + 1) * LANES]
                return c

            cnt = lax.fori_loop(0, n_kt, count_tile, jnp.zeros((TQ, LANES), jnp.int32))
            tot = jnp.sum(cnt, axis=1, keepdims=True)
            return jnp.where(tot >= topk, cand, thr)

        thr = lax.fori_loop(0, 32, bit_step, jnp.full((TQ, LANES), INT32_MIN, jnp.int32))
        thr_w = jnp.concatenate([thr] * reps, axis=1)

        def bias_tile(kt, carry):
            sel = (key_ref[kt] >= thr_w) & admissible(kt)
            bias = jnp.where(sel, 0.0, -jnp.inf).astype(F32)
            key_ref[kt] = pltpu.bitcast(bias, jnp.int32)
            return carry

        lax.fori_loop(0, n_kt, bias_tile, 0)

    qs = jnp.concatenate([q_ref[0, :, c * HEAD_DIM:(c + 1) * HEAD_DIM] for c in range(GROUP)], axis=0)
    m_ref[...] = jnp.full_like(m_ref, -jnp.inf)
    l_ref[...] = jnp.zeros_like(l_ref)
    acc_ref[...] = jnp.zeros_like(acc_ref)
    scale = HEAD_DIM ** -0.5

    def att_tile(kt, carry):
        s0 = pl.multiple_of(kt * TS, TS)
        kt_ = k_ref[0, pl.ds(s0, TS), :]
        vt_ = v_ref[0, pl.ds(s0, TS), :]
        s = lax.dot_general(qs, kt_, (((1,), (1,)), ((), ())), preferred_element_type=F32) * scale
        bias = pltpu.bitcast(key_ref[kt], F32)
        s = s + jnp.concatenate([bias] * GROUP, axis=0)
        m_old = m_ref[...]
        m_new = jnp.maximum(m_old, jnp.max(s, axis=1, keepdims=True))
        m_safe = jnp.where(m_new == -jnp.inf, 0.0, m_new)
        alpha = jnp.exp(m_old - m_safe)
        p = jnp.exp(s - m_safe)
        l_ref[...] = alpha * l_ref[...] + jnp.sum(p, axis=1, keepdims=True)
        acc_ref[...] = alpha * acc_ref[...] + jnp.dot(p.astype(BF16), vt_, preferred_element_type=F32)
        m_ref[...] = m_new
        return carry

    lax.fori_loop(0, n_kt, att_tile, 0)
    o = acc_ref[...] / l_ref[...]
    o_ref[0] = jnp.concatenate([o[c * TQ:(c + 1) * TQ] for c in range(GROUP)], axis=1).astype(o_ref.dtype)


def _dsa_attention(qkv, qi, wi, ki_even, ki_odd, *, n_heads, topk, TQ=256, TS=512):
    B, S, _ = qkv.shape
    n_kv = n_heads // GROUP
    NI = wi.shape[-1]
    TQ = _pick(S, TQ)
    TS = _pick(S, TS)
    n_tiles = S // TS
    kern = functools.partial(_dsa_kernel, TQ=TQ, TS=TS, NI=NI, topk=topk)
    gw = GROUP * HEAD_DIM
    return pl.pallas_call(
        kern,
        grid=(B, S // TQ, n_kv),
        in_specs=[
            pl.BlockSpec((1, TQ, gw), lambda b, i, g: (b, i, g)),
            pl.BlockSpec((1, S, HEAD_DIM), lambda b, i, g: (b, 0, n_heads + g)),
            pl.BlockSpec((1, S, HEAD_DIM), lambda b, i, g: (b, 0, n_heads + n_kv + g)),
            pl.BlockSpec((1, TQ, NI * IDX_HEAD_DIM), lambda b, i, g: (b, i, 0)),
            pl.BlockSpec((1, TQ, NI), lambda b, i, g: (b, i, 0)),
            pl.BlockSpec((1, S, LANES), lambda b, i, g: (b, 0, 0)),
            pl.BlockSpec((1, S, LANES), lambda b, i, g: (b, 0, 0)),
        ],
        out_specs=pl.BlockSpec((1, TQ, gw), lambda b, i, g: (b, i, g)),
        out_shape=jax.ShapeDtypeStruct((B, S, n_heads * HEAD_DIM), BF16),
        scratch_shapes=[
            pltpu.VMEM((n_tiles, TQ, TS), jnp.int32),
            pltpu.VMEM((NI, TQ, LANES), F32),
            pltpu.VMEM((GROUP * TQ, 1), F32),
            pltpu.VMEM((GROUP * TQ, 1), F32),
            pltpu.VMEM((GROUP * TQ, HEAD_DIM), F32),
        ],
        compiler_params=_compiler_params(("parallel", "parallel", "arbitrary")),
        name="dsa_attention",
    )(qkv, qkv, qkv, qi, wi, ki_even, ki_odd)


SUB = 16


def _hgrn_kernel(q_ref, f_ref, v_ref, gate_ref, ng_ref, o_ref, state_ref, *, TT, HG):
    @pl.when(pl.program_id(2) == 0)
    def _():
        state_ref[...] = jnp.zeros_like(state_ref)

    C = CHUNK
    row = lax.broadcasted_iota(jnp.int32, (C, C), 0)
    col = lax.broadcasted_iota(jnp.int32, (C, C), 1)
    tril = (row >= col).astype(F32)
    r1 = lax.broadcasted_iota(jnp.int32, (C, 1), 0)
    n_sub = C // SUB
    sub_row = lax.broadcasted_iota(jnp.int32, (SUB, C), 0)
    sub_col = lax.broadcasted_iota(jnp.int32, (SUB, C), 1)
    nt_dims = (((1,), (1,)), ((), ()))
    ng = ng_ref[...]

    def chunk_body(c, carry):
        r0 = pl.multiple_of(c * C, C)
        for h in range(HG):
            cols = slice(h * HGRN_EXPAND, (h + 1) * HGRN_EXPAND)
            q = q_ref[0, pl.ds(r0, C), cols]
            f = f_ref[0, pl.ds(r0, C), cols]
            v = v_ref[0, pl.ds(r0, C), cols]
            gate = gate_ref[0, pl.ds(r0, C), cols]
            lf = jnp.log(f)
            kin = 1.0 - f
            b = jnp.dot(tril, lf, preferred_element_type=F32, precision=lax.Precision.HIGHEST)
            v16 = v.astype(BF16)
            st = state_ref[h]
            o = lax.dot_general((q * jnp.exp(b)).astype(BF16), st.astype(BF16), nt_dims,
                                preferred_element_type=F32)
            qparts, kparts = [], []
            for i in range(1, n_sub):
                ref = b[i * SUB - 1:i * SUB, :]
                in_blk = (r1 >= i * SUB) & (r1 < (i + 1) * SUB)
                qparts.append(jnp.where(in_blk, q * jnp.exp(jnp.minimum(b - ref, 0.0)), 0.0))
                kparts.append(jnp.where(r1 < i * SUB, kin * jnp.exp(jnp.minimum(ref - b, 0.0)), 0.0))
            qcat = jnp.concatenate(qparts, axis=1).astype(BF16)
            kcat = jnp.concatenate(kparts, axis=1).astype(BF16)
            a_off = lax.dot_general(qcat, kcat, nt_dims, preferred_element_type=F32)
            dparts = []
            for i in range(n_sub):
                blk = slice(i * SUB, (i + 1) * SUB)
                bt, qt, kt_ = b[blk], q[blk], kin[blk]
                a = jnp.zeros((SUB, C), F32)
                for s in range(SUB):
                    e = jnp.exp(jnp.minimum(bt - bt[s:s + 1, :], 0.0))
                    a_col = jnp.sum(qt * kt_[s:s + 1, :] * e, axis=-1, keepdims=True)
                    a = jnp.where((sub_col == i * SUB + s) & (sub_row >= s), a_col, a)
                dparts.append(a)
            a_all = a_off + jnp.concatenate(dparts, axis=0)
            o = o + jnp.dot(a_all.astype(BF16), v16, preferred_element_type=F32)
            b_last = b[C - 1:C, :]
            kd = (kin * jnp.exp(b_last - b)).astype(BF16)
            upd = lax.dot_general(v16, kd, (((0,), (0,)), ((), ())), preferred_element_type=F32)
            state_ref[h] = jnp.exp(b_last) * st + upd
            ms = jnp.mean(o * o, axis=-1, keepdims=True)
            o_ref[0, pl.ds(r0, C), cols] = (o * lax.rsqrt(ms + RMS_EPS) * ng * gate).astype(o_ref.dtype)
        return carry

    lax.fori_loop(0, TT // C, chunk_body, 0)


def _hgrn_scan(q, f, v, gate, norm_g, *, TT=1024, HG=2):
    B, S, D = q.shape
    H = D // HGRN_EXPAND
    TT = _pick(S, TT)
    HG = _pick(H, HG)
    blk = pl.BlockSpec((1, TT, HG * HGRN_EXPAND), lambda b, h, t: (b, t, h))
    return pl.pallas_call(
        functools.partial(_hgrn_kernel, TT=TT, HG=HG),
        grid=(B, H // HG, S // TT),
        in_specs=[blk, blk, blk, blk, pl.BlockSpec((1, HGRN_EXPAND), lambda b, h, t: (0, 0))],
        out_specs=blk,
        out_shape=jax.ShapeDtypeStruct((B, S, D), BF16),
        scratch_shapes=[pltpu.VMEM((HG, HGRN_EXPAND, HGRN_EXPAND), F32)],
        compiler_params=_compiler_params(("parallel", "parallel", "arbitrary")),
        name="hgrn_scan",
    )(q, f, v, gate, norm_g.reshape(1, HGRN_EXPAND))


def _rope_tables(seq, dim):
    inv = 1.0 / (ROPE_THETA ** (jnp.arange(0, dim, 2, dtype=F32) / dim))
    ang = jnp.arange(seq, dtype=F32)[:, None] * inv[None, :]
    cos, sin = jnp.cos(ang), jnp.sin(ang)
    reps = LANES // dim
    cos_full = jnp.tile(jnp.concatenate([cos, cos], axis=-1), (1, reps))
    sin_signed = jnp.tile(jnp.concatenate([-sin, sin], axis=-1), (1, reps))
    return cos_full, sin_signed


def _dsa_mixer(xb, B, S, w_in, w_out, kln_g, kln_b):
    D = xb.shape[1]
    n_heads = D // HEAD_DIM
    n_kv = n_heads // GROUP
    d_qkv = (n_heads + 2 * n_kv) * HEAD_DIM
    NI = (w_in.shape[1] - d_qkv - IDX_HEAD_DIM) // (IDX_HEAD_DIM + 1)
    d_qi = NI * IDX_HEAD_DIM
    topk = min(TOPK_MAX, S // 4)
    cos128, sin128 = _rope_tables(S, HEAD_DIM)
    cos64, sin64 = _rope_tables(S, IDX_HEAD_DIM)

    w16 = w_in.astype(BF16)
    qkv = _matmul(xb, w16[:, :d_qkv], kind="qkv", out_dtype=BF16, row_aux=(cos128, sin128),
                  static_aux=(n_heads + n_kv,), seq=S)
    qi = _matmul(xb, w16[:, d_qkv:d_qkv + d_qi], kind="rope64", out_dtype=BF16,
                 row_aux=(cos64, sin64), seq=S)
    w_wi = w16[:, d_qkv + d_qi:d_qkv + d_qi + NI]
    w_ki = w16[:, d_qkv + d_qi + NI:]
    pad = LANES - IDX_HEAD_DIM - NI
    w_tail = jnp.concatenate([w_ki, w_wi, jnp.zeros((D, pad), BF16)], axis=1)
    g_pad = jnp.pad(kln_g.astype(F32), (0, LANES - IDX_HEAD_DIM)).reshape(1, LANES)
    b_pad = jnp.pad(kln_b.astype(F32), (0, LANES - IDX_HEAD_DIM)).reshape(1, LANES)
    tail = _matmul(xb, w_tail, kind="idx_tail", out_dtype=F32, row_aux=(cos64, sin64),
                   col_aux=(g_pad, b_pad), static_aux=(NI ** -0.5,), seq=S)
    ki = tail[:, :IDX_HEAD_DIM].astype(BF16).reshape(B, S, IDX_HEAD_DIM)
    wi = tail[:, IDX_HEAD_DIM:IDX_HEAD_DIM + NI].reshape(B, S, NI)
    ki_even = jnp.pad(ki, ((0, 0), (0, 0), (0, LANES - IDX_HEAD_DIM)))
    ki_odd = jnp.pad(ki, ((0, 0), (0, 0), (LANES - IDX_HEAD_DIM, 0)))

    o = _dsa_attention(qkv.reshape(B, S, d_qkv), qi.reshape(B, S, d_qi), wi, ki_even, ki_odd,
                       n_heads=n_heads, topk=topk)
    return _matmul(o.reshape(B * S, D), w_out.astype(BF16))


def _hgrn_mixer(xb, B, S, w_in, lb, norm_g, w_out):
    D = xb.shape[1]
    w16 = w_in.astype(BF16)
    q = _matmul(xb, w16[:, :D], kind="silu")
    f = _matmul(xb, w16[:, D:2 * D], kind="fgate", col_aux=(lb.reshape(1, D),))
    v = _matmul(xb, w16[:, 2 * D:3 * D])
    gate = _matmul(xb, w16[:, 3 * D:], kind="silu")
    shp = (B, S, D)
    o = _hgrn_scan(q.reshape(shp), f.reshape(shp), v.reshape(shp), gate.reshape(shp), norm_g)
    return _matmul(o.reshape(B * S, D), w_out.astype(BF16))


def _mlp(xb, w_up, w_down):
    h = _matmul(xb, w_up.astype(BF16), kind="relu2", out_dtype=BF16)
    return _matmul(h, w_down.astype(BF16), tk=1024)


def kernel(x, attn_w_in, attn_w_out, idx_k_ln_g, idx_k_ln_b, hgrn_w_in, hgrn_lower_bounds, hgrn_norm_g,
           hgrn_w_out, mix_ln_g, mix_ln_b, mlp_w_up, mlp_w_down, mlp_ln_g, mlp_ln_b):
    B, S, D = x.shape
    depth = mix_ln_g.shape[0]
    alpha = (2.0 * depth) ** 0.25
    lb_all = jnp.cumsum(jax.nn.softmax(hgrn_lower_bounds.astype(F32), axis=0), axis=0)
    lb_all = lb_all - lb_all[0:1]
    xf = x.reshape(B * S, D)
    xb = xf.astype(BF16)
    for layer in range(depth):
        j = layer // 2
        if layer % 2 == 0:
            h = _dsa_mixer(xb, B, S, attn_w_in[j], attn_w_out[j], idx_k_ln_g[j], idx_k_ln_b[j])
        else:
            h = _hgrn_mixer(xb, B, S, hgrn_w_in[j], lb_all[layer], hgrn_norm_g[j], hgrn_w_out[j])
        xf, xb = _residual_ln(xf, h, mix_ln_g[layer], mix_ln_b[layer], alpha)
        h = _mlp(xb, mlp_w_up[layer], mlp_w_down[layer])
        xf, xb = _residual_ln(xf, h, mlp_ln_g[layer], mlp_ln_b[layer], alpha)
    return xf.reshape(B, S, D)
```

```python
import functools
import math

import jax
import jax.numpy as jnp
from jax import lax
from jax.experimental import pallas as pl
from jax.experimental.pallas import tpu as pltpu

HEAD_DIM = 128
GROUP = 4
IDX_HEAD_DIM = 64
TOPK_MAX = 256
CHUNK = 64
ROPE_THETA = 10000.0
HGRN_EXPAND = 128
LN_EPS = 1e-5
RMS_EPS = 1e-6

V7X_VMEM_BYTES = 64 * 1024 * 1024
VMEM_LIMIT_BYTES = V7X_VMEM_BYTES * 7 // 8
LANES = 128

F32 = jnp.float32
BF16 = jnp.bfloat16
INT32_MIN = -(2 ** 31)


def _compiler_params(semantics):
    return pltpu.CompilerParams(dimension_semantics=semantics, vmem_limit_bytes=VMEM_LIMIT_BYTES)


def _pick(n, pref):
    t = min(n, pref)
    while n % t:
        t //= 2
    return t


def _rope128(x, cos, sin):
    return x * cos + pltpu.roll(x, 64, 1) * sin


def _rope64(x, cos, sin):
    lane = lax.broadcasted_iota(jnp.int32, x.shape, 1)
    first_half = (lane % 64) < 32
    partner = jnp.where(first_half, pltpu.roll(x, 96, 1), pltpu.roll(x, 32, 1))
    return x * cos + partner * sin


def _apply_epilogue(kind, acc, aux, j):
    tn = acc.shape[1]
    if kind == "none":
        return acc
    if kind == "relu2":
        r = jnp.maximum(acc, 0.0)
        return r * r
    if kind == "silu":
        return acc * jax.nn.sigmoid(acc)
    if kind == "fgate":
        lb = aux[0][...]
        return lb + (1.0 - lb) * jax.nn.sigmoid(acc)
    if kind in ("rope128", "rope64"):
        cos = aux[0][...]
        sin = aux[1][...]
        fn = _rope128 if kind == "rope128" else _rope64
        parts = [fn(acc[:, c * LANES:(c + 1) * LANES], cos, sin) for c in range(tn // LANES)]
        return jnp.concatenate(parts, axis=1)
    if kind == "qkv":
        cos = aux[0][...]
        sin = aux[1][...]
        n_rope = aux[2]
        parts = []
        for c in range(tn // LANES):
            x = acc[:, c * LANES:(c + 1) * LANES]
            parts.append(jnp.where(j * (tn // LANES) + c < n_rope, _rope128(x, cos, sin), x))
        return jnp.concatenate(parts, axis=1)
    if kind == "idx_tail":
        cos, sin, g, b, wscale = aux[0][...], aux[1][...], aux[2][...], aux[3][...], aux[4]
        lane = lax.broadcasted_iota(jnp.int32, acc.shape, 1)
        is_k = lane < IDX_HEAD_DIM
        mu = jnp.sum(jnp.where(is_k, acc, 0.0), axis=1, keepdims=True) * (1.0 / IDX_HEAD_DIM)
        d = jnp.where(is_k, acc - mu, 0.0)
        var = jnp.sum(d * d, axis=1, keepdims=True) * (1.0 / IDX_HEAD_DIM)
        y = d * lax.rsqrt(var + LN_EPS) * g + b
        y = _rope64(y, cos, sin)
        return jnp.where(is_k, y, acc * wscale)
    raise ValueError(kind)


def _mm_kernel(*refs, nk, kind, n_aux_refs, static_aux):
    a_ref, w_ref = refs[0], refs[1]
    aux_refs = list(refs[2:2 + n_aux_refs]) + list(static_aux)
    o_ref = refs[2 + n_aux_refs]
    j = pl.program_id(1)
    if nk == 1:
        acc = jnp.dot(a_ref[...], w_ref[...], preferred_element_type=F32)
        o_ref[...] = _apply_epilogue(kind, acc, aux_refs, j).astype(o_ref.dtype)
        return
    acc_ref = refs[3 + n_aux_refs]
    k = pl.program_id(2)

    @pl.when(k == 0)
    def _():
        acc_ref[...] = jnp.zeros_like(acc_ref)

    acc_ref[...] += jnp.dot(a_ref[...], w_ref[...], preferred_element_type=F32)

    @pl.when(k == nk - 1)
    def _():
        o_ref[...] = _apply_epilogue(kind, acc_ref[...], aux_refs, j).astype(o_ref.dtype)


def _matmul(a, w, *, kind="none", out_dtype=F32, row_aux=(), col_aux=(), static_aux=(),
            seq=None, tm=1024, tn=1024, tk=4096):
    M, K = a.shape
    N = w.shape[1]
    tm = _pick(M if seq is None else seq, tm)
    tn = _pick(N, tn)
    tk = _pick(K, tk)
    nk = K // tk
    grid = (M // tm, N // tn, nk)
    in_specs = [pl.BlockSpec((tm, tk), lambda i, j, k: (i, k)),
                pl.BlockSpec((tk, tn), lambda i, j, k: (k, j))]
    args = [a, w]
    for t in row_aux:
        nrow = seq // tm
        in_specs.append(pl.BlockSpec((tm, t.shape[1]), lambda i, j, k, nrow=nrow: (i % nrow, 0)))
        args.append(t)
    for t in col_aux:
        in_specs.append(pl.BlockSpec((1, tn), lambda i, j, k: (0, j)))
        args.append(t)
    scratch = [] if nk == 1 else [pltpu.VMEM((tm, tn), F32)]
    kern = functools.partial(_mm_kernel, nk=nk, kind=kind, n_aux_refs=len(row_aux) + len(col_aux),
                             static_aux=tuple(static_aux))
    return pl.pallas_call(
        kern,
        grid=grid,
        in_specs=in_specs,
        out_specs=pl.BlockSpec((tm, tn), lambda i, j, k: (i, j)),
        out_shape=jax.ShapeDtypeStruct((M, N), out_dtype),
        scratch_shapes=scratch,
        compiler_params=_compiler_params(("parallel", "parallel", "arbitrary")),
        name="mm_" + kind,
    )(*args)


def _ln_kernel(x_ref, h_ref, g_ref, b_ref, o_ref, ob_ref, *, alpha):
    y = alpha * x_ref[...] + h_ref[...]
    mu = jnp.mean(y, axis=-1, keepdims=True)
    d = y - mu
    var = jnp.mean(d * d, axis=-1, keepdims=True)
    out = d * lax.rsqrt(var + LN_EPS) * g_ref[...] + b_ref[...]
    o_ref[...] = out
    ob_ref[...] = out.astype(BF16)


def _residual_ln(x, h, g, b, alpha, tm=256):
    M, D = x.shape
    tm = _pick(M, tm)
    row = pl.BlockSpec((tm, D), lambda i: (i, 0))
    vec = pl.BlockSpec((1, D), lambda i: (0, 0))
    return pl.pallas_call(
        functools.partial(_ln_kernel, alpha=alpha),
        grid=(M // tm,),
        in_specs=[row, row, vec, vec],
        out_specs=[row, row],
        out_shape=[jax.ShapeDtypeStruct((M, D), F32), jax.ShapeDtypeStruct((M, D), BF16)],
        compiler_params=_compiler_params(("parallel",)),
        name="residual_ln",
    )(x, h, g.reshape(1, D), b.reshape(1, D))


PAIR = 2


COUNT_ROWS = 128


def _fori_unrolled(n, body, init, unroll):
    def many(i, carry):
        for u in range(unroll):
            carry = body(unroll * i + u, carry)
        return carry

    carry = lax.fori_loop(0, n // unroll, many, init)
    return lax.fori_loop((n // unroll) * unroll, n, body, carry)


def _for_tiles(n, body, unroll=4):
    def many(i, carry):
        for u in range(unroll):
            body(unroll * i + u)
        return carry

    lax.fori_loop(0, n // unroll, many, 0)
    width = unroll // 2
    while width >= 1:
        start = (n // (2 * width)) * (2 * width)

        @pl.when(n % (2 * width) >= width)
        def _(start=start, width=width):
            for u in range(width):
                body(start + u)

        width //= 2


def _dsa_kernel(q_ref, k_ref, v_ref, qi_ref, wi_ref, kk_ref, o_ref,
                key_ref, stash_ref, slab_ref, mp_ref, acc_ref, *, TQ, TS, NI, topk):
    qb = pl.program_id(1)
    g = pl.program_id(2)
    t0 = qb * TQ
    n_kt = (t0 + TQ + TS - 1) // TS
    reps = TS // LANES
    nt_dims = (((1,), (1,)), ((), ()))

    t_idx = t0 + lax.broadcasted_iota(jnp.int32, (TQ, 1), 0)
    chunk_end = (t_idx // CHUNK + 1) * CHUNK - 1

    def admissible(kt):
        s_idx = kt * TS + lax.broadcasted_iota(jnp.int32, (1, TS), 1)
        return s_idx <= chunk_end

    def wexp_at(h):
        return (h // (PAIR * reps), pl.ds(((h // reps) % PAIR) * TQ, TQ), pl.ds((h % reps) * LANES, LANES))

    @pl.when(g == 0)
    def _select():
        w_all = wi_ref[0] * (IDX_HEAD_DIM ** -0.5)
        lane = lax.broadcasted_iota(jnp.int32, (TQ, LANES), 1)
        for h in range(NI):
            stash_ref[wexp_at(h)] = jnp.broadcast_to(w_all[:, h:h + 1], (TQ, LANES))
            slab = qi_ref[0, :, (h // 2) * LANES:(h // 2 + 1) * LANES]
            keep = (lane < IDX_HEAD_DIM) if h % 2 == 0 else (lane >= IDX_HEAD_DIM)
            slab_ref[h] = jnp.where(keep, slab, jnp.zeros_like(slab))

        def score_tile(kt, carry):
            s0 = pl.multiple_of(kt * TS, TS)
            kk = kk_ref[0, pl.ds(s0, TS), :]
            acc = jnp.zeros((TQ, TS), F32)
            for h in range(NI):
                lg = lax.dot_general(slab_ref[h], kk, nt_dims, preferred_element_type=F32)
                w = jnp.concatenate([stash_ref[wexp_at(h)]] * reps, axis=1)
                acc = acc + w * jnp.maximum(lg, 0.0)
            sc = jnp.where(admissible(kt), acc, -jnp.inf)
            bits = pltpu.bitcast(sc, jnp.int32)
            key_ref[kt] = bits ^ ((bits >> 31) & jnp.int32(0x7FFFFFFF))
            return carry

        lax.fori_loop(0, n_kt, score_tile, 0)

        def bit_step(i, thr):
            cand = thr ^ (jnp.int32(1) << (31 - i))
            counts = []
            for rb in range(TQ // COUNT_ROWS):
                rows = pl.ds(rb * COUNT_ROWS, COUNT_ROWS)
                cand_rb = cand[rb * COUNT_ROWS:(rb + 1) * COUNT_ROWS]

                def count_tile(kt, c, rows=rows, cand_rb=cand_rb):
                    keys = key_ref[kt, rows, :]
                    for r in range(reps):
                        c = c + (keys[:, r * LANES:(r + 1) * LANES] >= cand_rb).astype(jnp.int32)
                    return c

                counts.append(_fori_unrolled(n_kt, count_tile, jnp.zeros((COUNT_ROWS, LANES), jnp.int32), 2))
            tot = jnp.sum(jnp.concatenate(counts, axis=0), axis=1, keepdims=True)
            return jnp.where(tot >= topk, cand, thr)

        thr = lax.fori_loop(0, 32, bit_step, jnp.full((TQ, LANES), INT32_MIN, jnp.int32))
        thr_w = jnp.concatenate([thr] * reps, axis=1)

        def bias_tile(kt, carry):
            sel = (key_ref[kt] >= thr_w) & admissible(kt)
            bias = jnp.where(sel, 0.0, -jnp.inf).astype(F32)
            key_ref[kt] = pltpu.bitcast(bias, jnp.int32)
            return carry

        lax.fori_loop(0, n_kt, bias_tile, 0)

    scale = HEAD_DIM ** -0.5
    ones_blk = jnp.ones((TS, HEAD_DIM), BF16)
    for pr in range(GROUP // PAIR):
        heads = range(pr * PAIR, (pr + 1) * PAIR)
        qs = jnp.concatenate([q_ref[0, :, c * HEAD_DIM:(c + 1) * HEAD_DIM] for c in heads], axis=0)
        mp_ref[...] = jnp.full_like(mp_ref, -jnp.inf)

        def logits_tile(kt):
            s0 = pl.multiple_of(kt * TS, TS)
            s = lax.dot_general(qs, k_ref[0, pl.ds(s0, TS), :], nt_dims, preferred_element_type=F32) * scale
            bias = pltpu.bitcast(key_ref[kt], F32)
            s = s + jnp.concatenate([bias] * PAIR, axis=0)
            stash_ref[kt] = s
            mp = mp_ref[...]
            for r in range(reps):
                mp = jnp.maximum(mp, s[:, r * LANES:(r + 1) * LANES])
            mp_ref[...] = mp

        _for_tiles(n_kt, logits_tile)
        m = jnp.max(mp_ref[...], axis=1, keepdims=True)
        m = jnp.where(m == -jnp.inf, 0.0, m)
        m_w = jnp.broadcast_to(m, (PAIR * TQ, TS))
        acc_ref[...] = jnp.zeros_like(acc_ref)

        def pv_tile(kt):
            s0 = pl.multiple_of(kt * TS, TS)
            p = jnp.exp(stash_ref[kt] - m_w).astype(BF16)
            v1 = jnp.concatenate([v_ref[0, pl.ds(s0, TS), :], ones_blk], axis=1)
            acc_ref[...] += jnp.dot(p, v1, preferred_element_type=F32)

        _for_tiles(n_kt, pv_tile)
        acc = acc_ref[...]
        o = acc[:, :HEAD_DIM] / acc[:, HEAD_DIM:]
        for i, c in enumerate(heads):
            o_ref[0, :, c * HEAD_DIM:(c + 1) * HEAD_DIM] = o[i * TQ:(i + 1) * TQ].astype(o_ref.dtype)


def _dsa_attention(qkv, qi, wi, kk, *, n_heads, topk, TQ=256, TS=512):
    B, S, _ = qkv.shape
    n_kv = n_heads // GROUP
    NI = wi.shape[-1]
    TQ = _pick(S, TQ)
    TS = _pick(S, TS)
    n_tiles = S // TS
    assert NI <= n_tiles * PAIR * (TS // LANES), "head-weight table does not fit in the logits stash"
    kern = functools.partial(_dsa_kernel, TQ=TQ, TS=TS, NI=NI, topk=topk)
    gw = GROUP * HEAD_DIM
    return pl.pallas_call(
        kern,
        grid=(B, S // TQ, n_kv),
        in_specs=[
            pl.BlockSpec((1, TQ, gw), lambda b, i, g: (b, i, g)),
            pl.BlockSpec((1, S, HEAD_DIM), lambda b, i, g: (b, 0, n_heads + g)),
            pl.BlockSpec((1, S, HEAD_DIM), lambda b, i, g: (b, 0, n_heads + n_kv + g)),
            pl.BlockSpec((1, TQ, NI * IDX_HEAD_DIM), lambda b, i, g: (b, i, 0)),
            pl.BlockSpec((1, TQ, NI), lambda b, i, g: (b, i, 0)),
            pl.BlockSpec((1, S, LANES), lambda b, i, g: (b, 0, 0)),
        ],
        out_specs=pl.BlockSpec((1, TQ, gw), lambda b, i, g: (b, i, g)),
        out_shape=jax.ShapeDtypeStruct((B, S, n_heads * HEAD_DIM), BF16),
        scratch_shapes=[
            pltpu.VMEM((n_tiles, TQ, TS), jnp.int32),
            pltpu.VMEM((n_tiles, PAIR * TQ, TS), F32),
            pltpu.VMEM((NI, TQ, LANES), BF16),
            pltpu.VMEM((PAIR * TQ, LANES), F32),
            pltpu.VMEM((PAIR * TQ, 2 * HEAD_DIM), F32),
        ],
        compiler_params=_compiler_params(("parallel", "parallel", "arbitrary")),
        name="dsa_attention",
    )(qkv, qkv, qkv, qi, wi, kk)


SUB = 16


def _hgrn_kernel(q_ref, f_ref, v_ref, gate_ref, ng_ref, o_ref, state_ref, *, TT, HG):
    @pl.when(pl.program_id(2) == 0)
    def _():
        state_ref[...] = jnp.zeros_like(state_ref)

    C = CHUNK
    row = lax.broadcasted_iota(jnp.int32, (C, C), 0)
    col = lax.broadcasted_iota(jnp.int32, (C, C), 1)
    tril = (row >= col).astype(F32)
    n_sub = C // SUB
    sub_row = lax.broadcasted_iota(jnp.int32, (SUB, C), 0)
    sub_col = lax.broadcasted_iota(jnp.int32, (SUB, C), 1)
    nt_dims = (((1,), (1,)), ((), ()))
    ng = ng_ref[...]

    def chunk_body(c, carry):
        r0 = pl.multiple_of(c * C, C)
        for h in range(HG):
            cols = slice(h * HGRN_EXPAND, (h + 1) * HGRN_EXPAND)
            q = q_ref[0, pl.ds(r0, C), cols]
            f = f_ref[0, pl.ds(r0, C), cols]
            v = v_ref[0, pl.ds(r0, C), cols]
            gate = gate_ref[0, pl.ds(r0, C), cols]
            lf = jnp.log(f)
            kin = 1.0 - f
            b = jnp.dot(tril, lf, preferred_element_type=F32, precision=lax.Precision.HIGHEST)
            v16 = v.astype(BF16)
            st = state_ref[h]
            o = lax.dot_general((q * jnp.exp(b)).astype(BF16), st.astype(BF16), nt_dims,
                                preferred_element_type=F32)
            qparts, kparts = [], []
            for i in range(1, n_sub):
                lo, hi = i * SUB, (i + 1) * SUB
                ref = b[lo - 1:lo, :]
                q_i = q[lo:hi] * jnp.exp(b[lo:hi] - ref)
                k_i = kin[:lo] * jnp.exp(ref - b[:lo])
                tail = [jnp.zeros((C - hi, HGRN_EXPAND), F32)] if hi < C else []
                qparts.append(jnp.concatenate([jnp.zeros((lo, HGRN_EXPAND), F32), q_i] + tail, axis=0))
                kparts.append(jnp.concatenate([k_i, jnp.zeros((C - lo, HGRN_EXPAND), F32)], axis=0))
            qcat = jnp.concatenate(qparts, axis=1).astype(BF16)
            kcat = jnp.concatenate(kparts, axis=1).astype(BF16)
            a_off = lax.dot_general(qcat, kcat, nt_dims, preferred_element_type=F32)
            dparts = []
            for i in range(n_sub):
                blk = slice(i * SUB, (i + 1) * SUB)
                bt, qt, kt_ = b[blk], q[blk], kin[blk]
                a = jnp.zeros((SUB, C), F32)
                for s in range(SUB):
                    e = jnp.exp(jnp.minimum(bt - bt[s:s + 1, :], 0.0))
                    a_col = jnp.sum(qt * kt_[s:s + 1, :] * e, axis=-1, keepdims=True)
                    a = jnp.where((sub_col == i * SUB + s) & (sub_row >= s), a_col, a)
                dparts.append(a)
            a_all = a_off + jnp.concatenate(dparts, axis=0)
            o = o + jnp.dot(a_all.astype(BF16), v16, preferred_element_type=F32)
            b_last = b[C - 1:C, :]
            kd = (kin * jnp.exp(b_last - b)).astype(BF16)
            upd = lax.dot_general(v16, kd, (((0,), (0,)), ((), ())), preferred_element_type=F32)
            state_ref[h] = jnp.exp(b_last) * st + upd
            ms = jnp.mean(o * o, axis=-1, keepdims=True)
            o_ref[0, pl.ds(r0, C), cols] = (o * lax.rsqrt(ms + RMS_EPS) * ng * gate).astype(o_ref.dtype)
        return carry

    lax.fori_loop(0, TT // C, chunk_body, 0)


def _hgrn_scan(q, f, v, gate, norm_g, *, TT=1024, HG=4):
    B, S, D = q.shape
    H = D // HGRN_EXPAND
    TT = _pick(S, TT)
    HG = _pick(H, HG)
    blk = pl.BlockSpec((1, TT, HG * HGRN_EXPAND), lambda b, h, t: (b, t, h))
    return pl.pallas_call(
        functools.partial(_hgrn_kernel, TT=TT, HG=HG),
        grid=(B, H // HG, S // TT),
        in_specs=[blk, blk, blk, blk, pl.BlockSpec((1, HGRN_EXPAND), lambda b, h, t: (0, 0))],
        out_specs=blk,
        out_shape=jax.ShapeDtypeStruct((B, S, D), BF16),
        scratch_shapes=[pltpu.VMEM((HG, HGRN_EXPAND, HGRN_EXPAND), F32)],
        compiler_params=_compiler_params(("parallel", "parallel", "arbitrary")),
        name="hgrn_scan",
    )(q, f, v, gate, norm_g.reshape(1, HGRN_EXPAND))


def _rope_tables(seq, dim):
    inv = 1.0 / (ROPE_THETA ** (jnp.arange(0, dim, 2, dtype=F32) / dim))
    ang = jnp.arange(seq, dtype=F32)[:, None] * inv[None, :]
    cos, sin = jnp.cos(ang), jnp.sin(ang)
    reps = LANES // dim
    cos_full = jnp.tile(jnp.concatenate([cos, cos], axis=-1), (1, reps))
    sin_signed = jnp.tile(jnp.concatenate([-sin, sin], axis=-1), (1, reps))
    return cos_full, sin_signed


def _dsa_mixer(xb, B, S, w_in, w_out, kln_g, kln_b):
    D = xb.shape[1]
    n_heads = D // HEAD_DIM
    n_kv = n_heads // GROUP
    d_qkv = (n_heads + 2 * n_kv) * HEAD_DIM
    NI = (w_in.shape[1] - d_qkv - IDX_HEAD_DIM) // (IDX_HEAD_DIM + 1)
    d_qi = NI * IDX_HEAD_DIM
    topk = min(TOPK_MAX, S // 4)
    cos128, sin128 = _rope_tables(S, HEAD_DIM)
    cos64, sin64 = _rope_tables(S, IDX_HEAD_DIM)

    w16 = w_in.astype(BF16)
    qkv = _matmul(xb, w16[:, :d_qkv], kind="qkv", out_dtype=BF16, row_aux=(cos128, sin128),
                  static_aux=(n_heads + n_kv,), seq=S)
    qi = _matmul(xb, w16[:, d_qkv:d_qkv + d_qi], kind="rope64", out_dtype=BF16,
                 row_aux=(cos64, sin64), seq=S)
    w_wi = w16[:, d_qkv + d_qi:d_qkv + d_qi + NI]
    w_ki = w16[:, d_qkv + d_qi + NI:]
    pad = LANES - IDX_HEAD_DIM - NI
    w_tail = jnp.concatenate([w_ki, w_wi, jnp.zeros((D, pad), BF16)], axis=1)
    g_pad = jnp.pad(kln_g.astype(F32), (0, LANES - IDX_HEAD_DIM)).reshape(1, LANES)
    b_pad = jnp.pad(kln_b.astype(F32), (0, LANES - IDX_HEAD_DIM)).reshape(1, LANES)
    tail = _matmul(xb, w_tail, kind="idx_tail", out_dtype=F32, row_aux=(cos64, sin64),
                   col_aux=(g_pad, b_pad), static_aux=(NI ** -0.5,), seq=S)
    ki = tail[:, :IDX_HEAD_DIM].astype(BF16).reshape(B, S, IDX_HEAD_DIM)
    wi = tail[:, IDX_HEAD_DIM:IDX_HEAD_DIM + NI].reshape(B, S, NI)
    kk = jnp.concatenate([ki, ki], axis=-1)

    o = _dsa_attention(qkv.reshape(B, S, d_qkv), qi.reshape(B, S, d_qi), wi, kk, n_heads=n_heads, topk=topk)
    return _matmul(o.reshape(B * S, D), w_out.astype(BF16))


def _hgrn_mixer(xb, B, S, w_in, lb, norm_g, w_out):
    D = xb.shape[1]
    w16 = w_in.astype(BF16)
    q = _matmul(xb, w16[:, :D], kind="silu")
    f = _matmul(xb, w16[:, D:2 * D], kind="fgate", col_aux=(lb.reshape(1, D),))
    v = _matmul(xb, w16[:, 2 * D:3 * D])
    gate = _matmul(xb, w16[:, 3 * D:], kind="silu")
    shp = (B, S, D)
    o = _hgrn_scan(q.reshape(shp), f.reshape(shp), v.reshape(shp), gate.reshape(shp), norm_g)
    return _matmul(o.reshape(B * S, D), w_out.astype(BF16))


def _mlp(xb, w_up, w_down):
    h = _matmul(xb, w_up.astype(BF16), kind="relu2", out_dtype=BF16)
    return _matmul(h, w_down.astype(BF16), tk=1024)


def kernel(x, attn_w_in, attn_w_out, idx_k_ln_g, idx_k_ln_b, hgrn_w_in, hgrn_lower_bounds, hgrn_norm_g,
           hgrn_w_out, mix_ln_g, mix_ln_b, mlp_w_up, mlp_w_down, mlp_ln_g, mlp_ln_b):
    B, S, D = x.shape
    depth = mix_ln_g.shape[0]
    alpha = (2.0 * depth) ** 0.25
    lb_all = jnp.cumsum(jax.nn.softmax(hgrn_lower_bounds.astype(F32), axis=0), axis=0)
    lb_all = lb_all - lb_all[0:1]
    xf = x.reshape(B * S, D)
    xb = xf.astype(BF16)
    for layer in range(depth):
        j = layer // 2
        if layer % 2 == 0:
            h = _dsa_mixer(xb, B, S, attn_w_in[j], attn_w_out[j], idx_k_ln_g[j], idx_k_ln_b[j])
        else:
            h = _hgrn_mixer(xb, B, S, hgrn_w_in[j], lb_all[layer], hgrn_norm_g[j], hgrn_w_out[j])
        xf, xb = _residual_ln(xf, h, mix_ln_g[layer], mix_ln_b[layer], alpha)
        h = _mlp(xb, mlp_w_up[layer], mlp_w_down[layer])
        xf, xb = _residual_ln(xf, h, mlp_ln_g[layer], mlp_ln_b[layer], alpha)
    return xf.reshape(B, S, D)
```

```python
import functools
import math

import jax
import jax.numpy as jnp
from jax import lax
from jax.experimental import pallas as pl
from jax.experimental.pallas import tpu as pltpu

HEAD_DIM = 128
GROUP = 4
IDX_HEAD_DIM = 64
TOPK_MAX = 256
CHUNK = 64
ROPE_THETA = 10000.0
HGRN_EXPAND = 128
LN_EPS = 1e-5
RMS_EPS = 1e-6

V7X_VMEM_BYTES = 64 * 1024 * 1024
VMEM_LIMIT_BYTES = V7X_VMEM_BYTES * 7 // 8
LANES = 128

F32 = jnp.float32
BF16 = jnp.bfloat16
INT32_MIN = -(2 ** 31)


def _compiler_params(semantics):
    return pltpu.CompilerParams(dimension_semantics=semantics, vmem_limit_bytes=VMEM_LIMIT_BYTES)


def _pick(n, pref):
    t = min(n, pref)
    while n % t:
        t //= 2
    return t


def _rope128(x, cos, sin):
    return x * cos + pltpu.roll(x, 64, 1) * sin


def _rope64(x, cos, sin):
    lane = lax.broadcasted_iota(jnp.int32, x.shape, 1)
    first_half = (lane % 64) < 32
    partner = jnp.where(first_half, pltpu.roll(x, 96, 1), pltpu.roll(x, 32, 1))
    return x * cos + partner * sin


def _apply_epilogue(kind, acc, aux, j):
    tn = acc.shape[1]
    if kind == "none":
        return acc
    if kind == "relu2":
        r = jnp.maximum(acc, 0.0)
        return r * r
    if kind == "silu":
        return acc * jax.nn.sigmoid(acc)
    if kind == "fgate":
        lb = aux[0][...]
        return lb + (1.0 - lb) * jax.nn.sigmoid(acc)
    if kind in ("rope128", "rope64"):
        cos = aux[0][...]
        sin = aux[1][...]
        fn = _rope128 if kind == "rope128" else _rope64
        parts = [fn(acc[:, c * LANES:(c + 1) * LANES], cos, sin) for c in range(tn // LANES)]
        return jnp.concatenate(parts, axis=1)
    if kind == "qkv":
        cos = aux[0][...]
        sin = aux[1][...]
        n_rope = aux[2]
        parts = []
        for c in range(tn // LANES):
            x = acc[:, c * LANES:(c + 1) * LANES]
            parts.append(jnp.where(j * (tn // LANES) + c < n_rope, _rope128(x, cos, sin), x))
        return jnp.concatenate(parts, axis=1)
    if kind == "idx_tail":
        cos, sin, g, b, wscale = aux[0][...], aux[1][...], aux[2][...], aux[3][...], aux[4]
        lane = lax.broadcasted_iota(jnp.int32, acc.shape, 1)
        is_k = lane < IDX_HEAD_DIM
        mu = jnp.sum(jnp.where(is_k, acc, 0.0), axis=1, keepdims=True) * (1.0 / IDX_HEAD_DIM)
        d = jnp.where(is_k, acc - mu, 0.0)
        var = jnp.sum(d * d, axis=1, keepdims=True) * (1.0 / IDX_HEAD_DIM)
        y = d * lax.rsqrt(var + LN_EPS) * g + b
        y = _rope64(y, cos, sin)
        return jnp.where(is_k, y, acc * wscale)
    raise ValueError(kind)


def _mm_kernel(*refs, nk, kind, n_aux_refs, static_aux):
    a_ref, w_ref = refs[0], refs[1]
    aux_refs = list(refs[2:2 + n_aux_refs]) + list(static_aux)
    o_ref = refs[2 + n_aux_refs]
    j = pl.program_id(1)
    if nk == 1:
        acc = jnp.dot(a_ref[...], w_ref[...], preferred_element_type=F32)
        o_ref[...] = _apply_epilogue(kind, acc, aux_refs, j).astype(o_ref.dtype)
        return
    acc_ref = refs[3 + n_aux_refs]
    k = pl.program_id(2)

    @pl.when(k == 0)
    def _():
        acc_ref[...] = jnp.zeros_like(acc_ref)

    acc_ref[...] += jnp.dot(a_ref[...], w_ref[...], preferred_element_type=F32)

    @pl.when(k == nk - 1)
    def _():
        o_ref[...] = _apply_epilogue(kind, acc_ref[...], aux_refs, j).astype(o_ref.dtype)


def _matmul(a, w, *, kind="none", out_dtype=F32, row_aux=(), col_aux=(), static_aux=(),
            seq=None, cols=None, tm=1024, tn=1024, tk=4096):
    M, K = a.shape
    lo, hi = (0, w.shape[1]) if cols is None else cols
    N = hi - lo
    tm = _pick(M if seq is None else seq, tm)
    tn = _pick(math.gcd(N, lo) if lo else N, tn)
    tk = _pick(K, tk)
    nk = K // tk
    j0 = lo // tn
    grid = (M // tm, N // tn, nk)
    in_specs = [pl.BlockSpec((tm, tk), lambda i, j, k: (i, k)),
                pl.BlockSpec((tk, tn), lambda i, j, k: (k, j + j0))]
    args = [a, w]
    for t in row_aux:
        nrow = seq // tm
        in_specs.append(pl.BlockSpec((tm, t.shape[1]), lambda i, j, k, nrow=nrow: (i % nrow, 0)))
        args.append(t)
    for t in col_aux:
        in_specs.append(pl.BlockSpec((1, tn), lambda i, j, k: (0, j)))
        args.append(t)
    scratch = [] if nk == 1 else [pltpu.VMEM((tm, tn), F32)]
    kern = functools.partial(_mm_kernel, nk=nk, kind=kind, n_aux_refs=len(row_aux) + len(col_aux),
                             static_aux=tuple(static_aux))
    return pl.pallas_call(
        kern,
        grid=grid,
        in_specs=in_specs,
        out_specs=pl.BlockSpec((tm, tn), lambda i, j, k: (i, j)),
        out_shape=jax.ShapeDtypeStruct((M, N), out_dtype),
        scratch_shapes=scratch,
        compiler_params=_compiler_params(("parallel", "parallel", "arbitrary")),
        name="mm_" + kind,
    )(*args)


def _ln_kernel(x_ref, h_ref, g_ref, b_ref, o_ref, ob_ref, *, alpha):
    y = alpha * x_ref[...] + h_ref[...]
    mu = jnp.mean(y, axis=-1, keepdims=True)
    d = y - mu
    var = jnp.mean(d * d, axis=-1, keepdims=True)
    out = d * lax.rsqrt(var + LN_EPS) * g_ref[...] + b_ref[...]
    o_ref[...] = out
    ob_ref[...] = out.astype(BF16)


def _residual_ln(x, h, g, b, alpha, tm=256):
    M, D = x.shape
    tm = _pick(M, tm)
    row = pl.BlockSpec((tm, D), lambda i: (i, 0))
    vec = pl.BlockSpec((1, D), lambda i: (0, 0))
    return pl.pallas_call(
        functools.partial(_ln_kernel, alpha=alpha),
        grid=(M // tm,),
        in_specs=[row, row, vec, vec],
        out_specs=[row, row],
        out_shape=[jax.ShapeDtypeStruct((M, D), F32), jax.ShapeDtypeStruct((M, D), BF16)],
        compiler_params=_compiler_params(("parallel",)),
        name="residual_ln",
    )(x, h, g.reshape(1, D), b.reshape(1, D))


PAIR = 2


COUNT_ROWS = 128


def _fori_unrolled(n, body, init, unroll):
    def many(i, carry):
        for u in range(unroll):
            carry = body(unroll * i + u, carry)
        return carry

    carry = lax.fori_loop(0, n // unroll, many, init)
    return lax.fori_loop((n // unroll) * unroll, n, body, carry)


def _for_tiles(n, body, unroll=4):
    def many(i, carry):
        for u in range(unroll):
            body(unroll * i + u)
        return carry

    lax.fori_loop(0, n // unroll, many, 0)
    width = unroll // 2
    while width >= 1:
        start = (n // (2 * width)) * (2 * width)

        @pl.when(n % (2 * width) >= width)
        def _(start=start, width=width):
            for u in range(width):
                body(start + u)

        width //= 2


def _dsa_kernel(q_ref, k_ref, v_ref, qi_ref, wi_ref, kk_ref, o_ref,
                key_ref, stash_ref, slab_ref, mp_ref, acc_ref, *, TQ, TS, NI, topk, idx_bits):
    qb = pl.program_id(1)
    g = pl.program_id(2)
    t0 = qb * TQ
    n_kt = (t0 + TQ + TS - 1) // TS
    reps = TS // LANES
    nt_dims = (((1,), (1,)), ((), ()))

    t_idx = t0 + lax.broadcasted_iota(jnp.int32, (TQ, 1), 0)
    chunk_end = (t_idx // CHUNK + 1) * CHUNK - 1

    def admissible(kt):
        s_idx = kt * TS + lax.broadcasted_iota(jnp.int32, (1, TS), 1)
        return s_idx <= chunk_end

    def wexp_at(h):
        return (h // (PAIR * reps), pl.ds(((h // reps) % PAIR) * TQ, TQ), pl.ds((h % reps) * LANES, LANES))

    @pl.when(g == 0)
    def _select():
        w_all = wi_ref[0] * (IDX_HEAD_DIM ** -0.5)
        lane = lax.broadcasted_iota(jnp.int32, (TQ, LANES), 1)
        for h in range(NI):
            stash_ref[wexp_at(h)] = jnp.broadcast_to(w_all[:, h:h + 1], (TQ, LANES))
            slab = qi_ref[0, :, (h // 2) * LANES:(h // 2 + 1) * LANES]
            keep = (lane < IDX_HEAD_DIM) if h % 2 == 0 else (lane >= IDX_HEAD_DIM)
            slab_ref[h] = jnp.where(keep, slab, jnp.zeros_like(slab))

        def score_tile(kt, carry):
            s0 = pl.multiple_of(kt * TS, TS)
            kk = kk_ref[0, pl.ds(s0, TS), :]
            acc = jnp.zeros((TQ, TS), F32)
            for h in range(NI):
                lg = lax.dot_general(slab_ref[h], kk, nt_dims, preferred_element_type=F32)
                w = jnp.concatenate([stash_ref[wexp_at(h)]] * reps, axis=1)
                acc = acc + w * jnp.maximum(lg, 0.0)
            sc = jnp.where(admissible(kt), acc, -jnp.inf)
            bits = pltpu.bitcast(sc, jnp.int32)
            key_ref[kt] = bits ^ ((bits >> 31) & jnp.int32(0x7FFFFFFF))
            return carry

        lax.fori_loop(0, n_kt, score_tile, 0)

        def count_keys(pred):
            lane_idx = lax.broadcasted_iota(jnp.int32, (COUNT_ROWS, LANES), 1)
            counts = []
            for rb in range(TQ // COUNT_ROWS):
                rows = pl.ds(rb * COUNT_ROWS, COUNT_ROWS)

                def count_tile(kt, c, rows=rows, rb=rb):
                    keys = key_ref[kt, rows, :]
                    for r in range(reps):
                        s_idx = kt * TS + r * LANES + lane_idx
                        c = c + pred(keys[:, r * LANES:(r + 1) * LANES], s_idx, rb).astype(jnp.int32)
                    return c

                counts.append(_fori_unrolled(n_kt, count_tile, jnp.zeros((COUNT_ROWS, LANES), jnp.int32), 2))
            return jnp.sum(jnp.concatenate(counts, axis=0), axis=1, keepdims=True)

        def rows_of(x, rb):
            return x[rb * COUNT_ROWS:(rb + 1) * COUNT_ROWS]

        def bit_step(i, thr):
            cand = thr ^ (jnp.int32(1) << (31 - i))
            tot = count_keys(lambda keys, s_idx, rb: keys >= rows_of(cand, rb))
            return jnp.where(tot >= topk, cand, thr)

        thr = lax.fori_loop(0, 32, bit_step, jnp.full((TQ, LANES), INT32_MIN, jnp.int32))

        n_ge = count_keys(lambda keys, s_idx, rb: keys >= rows_of(thr, rb))

        def tie_break():
            n_gt = count_keys(lambda keys, s_idx, rb: keys > rows_of(thr, rb))
            need = topk - n_gt

            def idx_step(i, last):
                cand = last | (jnp.int32(1) << (idx_bits - 1 - i))
                below = count_keys(
                    lambda keys, s_idx, rb: (keys == rows_of(thr, rb)) & (s_idx < rows_of(cand, rb)))
                return jnp.where(below < need, cand, last)

            return lax.fori_loop(0, idx_bits, idx_step, jnp.zeros((TQ, LANES), jnp.int32))

        last_tie = lax.cond(jnp.max(n_ge) > topk, tie_break,
                            lambda: jnp.full((TQ, LANES), 2 ** 31 - 1, jnp.int32))
        thr_w = jnp.concatenate([thr] * reps, axis=1)
        last_w = jnp.concatenate([last_tie] * reps, axis=1)

        def bias_tile(kt, carry):
            keys = key_ref[kt]
            s_idx = kt * TS + lax.broadcasted_iota(jnp.int32, (TQ, TS), 1)
            sel = ((keys > thr_w) | ((keys == thr_w) & (s_idx <= last_w))) & admissible(kt)
            bias = jnp.where(sel, 0.0, -jnp.inf).astype(F32)
            key_ref[kt] = pltpu.bitcast(bias, jnp.int32)
            return carry

        lax.fori_loop(0, n_kt, bias_tile, 0)

    scale = HEAD_DIM ** -0.5
    ones_blk = jnp.ones((TS, HEAD_DIM), BF16)
    for pr in range(GROUP // PAIR):
        heads = range(pr * PAIR, (pr + 1) * PAIR)
        qs = jnp.concatenate([q_ref[0, :, c * HEAD_DIM:(c + 1) * HEAD_DIM] for c in heads], axis=0)
        mp_ref[...] = jnp.full_like(mp_ref, -jnp.inf)

        def logits_tile(kt):
            s0 = pl.multiple_of(kt * TS, TS)
            s = lax.dot_general(qs, k_ref[0, pl.ds(s0, TS), :], nt_dims, preferred_element_type=F32) * scale
            bias = pltpu.bitcast(key_ref[kt], F32)
            s = s + jnp.concatenate([bias] * PAIR, axis=0)
            stash_ref[kt] = s
            mp = mp_ref[...]
            for r in range(reps):
                mp = jnp.maximum(mp, s[:, r * LANES:(r + 1) * LANES])
            mp_ref[...] = mp

        _for_tiles(n_kt, logits_tile)
        m = jnp.max(mp_ref[...], axis=1, keepdims=True)
        m = jnp.where(m == -jnp.inf, 0.0, m)
        m_w = jnp.broadcast_to(m, (PAIR * TQ, TS))
        acc_ref[...] = jnp.zeros_like(acc_ref)

        def pv_tile(kt):
            s0 = pl.multiple_of(kt * TS, TS)
            p = jnp.exp(stash_ref[kt] - m_w).astype(BF16)
            v1 = jnp.concatenate([v_ref[0, pl.ds(s0, TS), :], ones_blk], axis=1)
            acc_ref[...] += jnp.dot(p, v1, preferred_element_type=F32)

        _for_tiles(n_kt, pv_tile)
        acc = acc_ref[...]
        o = acc[:, :HEAD_DIM] / acc[:, HEAD_DIM:]
        for i, c in enumerate(heads):
            o_ref[0, :, c * HEAD_DIM:(c + 1) * HEAD_DIM] = o[i * TQ:(i + 1) * TQ].astype(o_ref.dtype)


def _dsa_attention(qkv, qi, wi, kk, *, n_heads, topk, TQ=256, TS=512):
    B, S, _ = qkv.shape
    n_kv = n_heads // GROUP
    NI = wi.shape[-1]
    TQ = _pick(S, TQ)
    TS = _pick(S, TS)
    n_tiles = S // TS
    assert NI <= n_tiles * PAIR * (TS // LANES), "head-weight table does not fit in the logits stash"
    kern = functools.partial(_dsa_kernel, TQ=TQ, TS=TS, NI=NI, topk=topk, idx_bits=(S - 1).bit_length())
    gw = GROUP * HEAD_DIM
    return pl.pallas_call(
        kern,
        grid=(B, S // TQ, n_kv),
        in_specs=[
            pl.BlockSpec((1, TQ, gw), lambda b, i, g: (b, i, g)),
            pl.BlockSpec((1, S, HEAD_DIM), lambda b, i, g: (b, 0, n_heads + g)),
            pl.BlockSpec((1, S, HEAD_DIM), lambda b, i, g: (b, 0, n_heads + n_kv + g)),
            pl.BlockSpec((1, TQ, NI * IDX_HEAD_DIM), lambda b, i, g: (b, i, 0)),
            pl.BlockSpec((1, TQ, NI), lambda b, i, g: (b, i, 0)),
            pl.BlockSpec((1, S, LANES), lambda b, i, g: (b, 0, 0)),
        ],
        out_specs=pl.BlockSpec((1, TQ, gw), lambda b, i, g: (b, i, g)),
        out_shape=jax.ShapeDtypeStruct((B, S, n_heads * HEAD_DIM), BF16),
        scratch_shapes=[
            pltpu.VMEM((n_tiles, TQ, TS), jnp.int32),
            pltpu.VMEM((n_tiles, PAIR * TQ, TS), F32),
            pltpu.VMEM((NI, TQ, LANES), BF16),
            pltpu.VMEM((PAIR * TQ, LANES), F32),
            pltpu.VMEM((PAIR * TQ, 2 * HEAD_DIM), F32),
        ],
        compiler_params=_compiler_params(("parallel", "parallel", "arbitrary")),
        name="dsa_attention",
    )(qkv, qkv, qkv, qi, wi, kk)


SUB = 16


def _hgrn_kernel(q_ref, f_ref, v_ref, gate_ref, ng_ref, o_ref, state_ref, *, TT, HG):
    @pl.when(pl.program_id(2) == 0)
    def _():
        state_ref[...] = jnp.zeros_like(state_ref)

    C = CHUNK
    row = lax.broadcasted_iota(jnp.int32, (C, C), 0)
    col = lax.broadcasted_iota(jnp.int32, (C, C), 1)
    tril = (row >= col).astype(F32)
    n_sub = C // SUB
    sub_row = lax.broadcasted_iota(jnp.int32, (SUB, C), 0)
    sub_col = lax.broadcasted_iota(jnp.int32, (SUB, C), 1)
    nt_dims = (((1,), (1,)), ((), ()))
    ng = ng_ref[...]

    def chunk_body(c, carry):
        r0 = pl.multiple_of(c * C, C)
        for h in range(HG):
            cols = slice(h * HGRN_EXPAND, (h + 1) * HGRN_EXPAND)
            q = q_ref[0, pl.ds(r0, C), cols]
            f = f_ref[0, pl.ds(r0, C), cols]
            v = v_ref[0, pl.ds(r0, C), cols]
            gate = gate_ref[0, pl.ds(r0, C), cols]
            lf = jnp.log(f)
            kin = 1.0 - f
            b = jnp.dot(tril, lf, preferred_element_type=F32, precision=lax.Precision.HIGHEST)
            v16 = v.astype(BF16)
            st = state_ref[h]
            o = lax.dot_general((q * jnp.exp(b)).astype(BF16), st.astype(BF16), nt_dims,
                                preferred_element_type=F32)
            qparts, kparts = [], []
            for i in range(1, n_sub):
                lo, hi = i * SUB, (i + 1) * SUB
                ref = b[lo - 1:lo, :]
                q_i = q[lo:hi] * jnp.exp(b[lo:hi] - ref)
                k_i = kin[:lo] * jnp.exp(ref - b[:lo])
                tail = [jnp.zeros((C - hi, HGRN_EXPAND), F32)] if hi < C else []
                qparts.append(jnp.concatenate([jnp.zeros((lo, HGRN_EXPAND), F32), q_i] + tail, axis=0))
                kparts.append(jnp.concatenate([k_i, jnp.zeros((C - lo, HGRN_EXPAND), F32)], axis=0))
            qcat = jnp.concatenate(qparts, axis=1).astype(BF16)
            kcat = jnp.concatenate(kparts, axis=1).astype(BF16)
            a_off = lax.dot_general(qcat, kcat, nt_dims, preferred_element_type=F32)
            dparts = []
            for i in range(n_sub):
                blk = slice(i * SUB, (i + 1) * SUB)
                bt, qt, kt_ = b[blk], q[blk], kin[blk]
                a = jnp.zeros((SUB, C), F32)
                for s in range(SUB):
                    e = jnp.exp(bt - bt[s:s + 1, :])
                    a_col = jnp.sum(qt * kt_[s:s + 1, :] * e, axis=-1, keepdims=True)
                    a = jnp.where((sub_col == i * SUB + s) & (sub_row >= s), a_col, a)
                dparts.append(a)
            a_all = a_off + jnp.concatenate(dparts, axis=0)
            o = o + jnp.dot(a_all.astype(BF16), v16, preferred_element_type=F32)
            b_last = b[C - 1:C, :]
            kd = (kin * jnp.exp(b_last - b)).astype(BF16)
            upd = lax.dot_general(v16, kd, (((0,), (0,)), ((), ())), preferred_element_type=F32)
            state_ref[h] = jnp.exp(b_last) * st + upd
            ms = jnp.mean(o * o, axis=-1, keepdims=True)
            o_ref[0, pl.ds(r0, C), cols] = (o * lax.rsqrt(ms + RMS_EPS) * ng * gate).astype(o_ref.dtype)
        return carry

    lax.fori_loop(0, TT // C, chunk_body, 0)


def _hgrn_scan(q, f, v, gate, norm_g, *, TT=1024, HG=4):
    B, S, D = q.shape
    H = D // HGRN_EXPAND
    TT = _pick(S, TT)
    HG = _pick(H, HG)
    blk = pl.BlockSpec((1, TT, HG * HGRN_EXPAND), lambda b, h, t: (b, t, h))
    return pl.pallas_call(
        functools.partial(_hgrn_kernel, TT=TT, HG=HG),
        grid=(B, H // HG, S // TT),
        in_specs=[blk, blk, blk, blk, pl.BlockSpec((1, HGRN_EXPAND), lambda b, h, t: (0, 0))],
        out_specs=blk,
        out_shape=jax.ShapeDtypeStruct((B, S, D), BF16),
        scratch_shapes=[pltpu.VMEM((HG, HGRN_EXPAND, HGRN_EXPAND), F32)],
        compiler_params=_compiler_params(("parallel", "parallel", "arbitrary")),
        name="hgrn_scan",
    )(q, f, v, gate, norm_g.reshape(1, HGRN_EXPAND))


def _rope_tables(seq, dim):
    inv = 1.0 / (ROPE_THETA ** (jnp.arange(0, dim, 2, dtype=F32) / dim))
    ang = jnp.arange(seq, dtype=F32)[:, None] * inv[None, :]
    cos, sin = jnp.cos(ang), jnp.sin(ang)
    reps = LANES // dim
    cos_full = jnp.tile(jnp.concatenate([cos, cos], axis=-1), (1, reps))
    sin_signed = jnp.tile(jnp.concatenate([-sin, sin], axis=-1), (1, reps))
    return cos_full, sin_signed


def _dsa_mixer(xb, B, S, w_in, w_out, kln_g, kln_b):
    D = xb.shape[1]
    n_heads = D // HEAD_DIM
    n_kv = n_heads // GROUP
    d_qkv = (n_heads + 2 * n_kv) * HEAD_DIM
    NI = (w_in.shape[1] - d_qkv - IDX_HEAD_DIM) // (IDX_HEAD_DIM + 1)
    d_qi = NI * IDX_HEAD_DIM
    topk = min(TOPK_MAX, S // 4)
    cos128, sin128 = _rope_tables(S, HEAD_DIM)
    cos64, sin64 = _rope_tables(S, IDX_HEAD_DIM)

    w16 = w_in.astype(BF16)
    qkv = _matmul(xb, w16, cols=(0, d_qkv), kind="qkv", out_dtype=BF16, row_aux=(cos128, sin128),
                  static_aux=(n_heads + n_kv,), seq=S)
    qi = _matmul(xb, w16, cols=(d_qkv, d_qkv + d_qi), kind="rope64", out_dtype=BF16,
                 row_aux=(cos64, sin64), seq=S)
    w_wi = w16[:, d_qkv + d_qi:d_qkv + d_qi + NI]
    w_ki = w16[:, d_qkv + d_qi + NI:]
    pad = LANES - IDX_HEAD_DIM - NI
    w_tail = jnp.concatenate([w_ki, w_wi, jnp.zeros((D, pad), BF16)], axis=1)
    g_pad = jnp.pad(kln_g.astype(F32), (0, LANES - IDX_HEAD_DIM)).reshape(1, LANES)
    b_pad = jnp.pad(kln_b.astype(F32), (0, LANES - IDX_HEAD_DIM)).reshape(1, LANES)
    tail = _matmul(xb, w_tail, kind="idx_tail", out_dtype=F32, row_aux=(cos64, sin64),
                   col_aux=(g_pad, b_pad), static_aux=(NI ** -0.5,), seq=S)
    ki = tail[:, :IDX_HEAD_DIM].astype(BF16).reshape(B, S, IDX_HEAD_DIM)
    wi = tail[:, IDX_HEAD_DIM:IDX_HEAD_DIM + NI].reshape(B, S, NI)
    kk = jnp.concatenate([ki, ki], axis=-1)

    o = _dsa_attention(qkv.reshape(B, S, d_qkv), qi.reshape(B, S, d_qi), wi, kk, n_heads=n_heads, topk=topk)
    return _matmul(o.reshape(B * S, D), w_out.astype(BF16))


def _hgrn_mixer(xb, B, S, w_in, lb, norm_g, w_out):
    D = xb.shape[1]
    w16 = w_in.astype(BF16)
    q = _matmul(xb, w16, cols=(0, D), kind="silu")
    f = _matmul(xb, w16, cols=(D, 2 * D), kind="fgate", col_aux=(lb.reshape(1, D),))
    v = _matmul(xb, w16, cols=(2 * D, 3 * D))
    gate = _matmul(xb, w16, cols=(3 * D, 4 * D), kind="silu")
    shp = (B, S, D)
    o = _hgrn_scan(q.reshape(shp), f.reshape(shp), v.reshape(shp), gate.reshape(shp), norm_g)
    return _matmul(o.reshape(B * S, D), w_out.astype(BF16))


def _mlp(xb, w_up, w_down):
    h = _matmul(xb, w_up.astype(BF16), kind="relu2", out_dtype=BF16)
    return _matmul(h, w_down.astype(BF16))


def kernel(x, attn_w_in, attn_w_out, idx_k_ln_g, idx_k_ln_b, hgrn_w_in, hgrn_lower_bounds, hgrn_norm_g,
           hgrn_w_out, mix_ln_g, mix_ln_b, mlp_w_up, mlp_w_down, mlp_ln_g, mlp_ln_b):
    B, S, D = x.shape
    depth = mix_ln_g.shape[0]
    alpha = (2.0 * depth) ** 0.25
    lb_all = jnp.cumsum(jax.nn.softmax(hgrn_lower_bounds.astype(F32), axis=0), axis=0)
    lb_all = lb_all - lb_all[0:1]
    xf = x.reshape(B * S, D)
    xb = xf.astype(BF16)
    for layer in range(depth):
        j = layer // 2
        if layer % 2 == 0:
            h = _dsa_mixer(xb, B, S, attn_w_in[j], attn_w_out[j], idx_k_ln_g[j], idx_k_ln_b[j])
        else:
            h = _hgrn_mixer(xb, B, S, hgrn_w_in[j], lb_all[layer], hgrn_norm_g[j], hgrn_w_out[j])
        xf, xb = _residual_ln(xf, h, mix_ln_g[layer], mix_ln_b[layer], alpha)
        h = _mlp(xb, mlp_w_up[layer], mlp_w_down[layer])
        xf, xb = _residual_ln(xf, h, mlp_ln_g[layer], mlp_ln_b[layer], alpha)
    return xf.reshape(B, S, D)
```

```python
import functools
import math

import jax
import jax.numpy as jnp
from jax import lax
from jax.experimental import pallas as pl
from jax.experimental.pallas import tpu as pltpu

HEAD_DIM = 128
GROUP = 4
IDX_HEAD_DIM = 64
TOPK_MAX = 256
CHUNK = 64
ROPE_THETA = 10000.0
HGRN_EXPAND = 128
LN_EPS = 1e-5
RMS_EPS = 1e-6

V7X_VMEM_BYTES = 64 * 1024 * 1024
VMEM_LIMIT_BYTES = V7X_VMEM_BYTES * 7 // 8
LANES = 128

F32 = jnp.float32
BF16 = jnp.bfloat16
INT32_MIN = -(2 ** 31)


def _compiler_params(semantics):
    return pltpu.CompilerParams(dimension_semantics=semantics, vmem_limit_bytes=VMEM_LIMIT_BYTES)


def _pick(n, pref):
    t = min(n, pref)
    while n % t:
        t //= 2
    return t


def _rope128(x, cos, sin):
    return x * cos + pltpu.roll(x, 64, 1) * sin


def _rope64(x, cos, sin):
    lane = lax.broadcasted_iota(jnp.int32, x.shape, 1)
    first_half = (lane % 64) < 32
    partner = jnp.where(first_half, pltpu.roll(x, 96, 1), pltpu.roll(x, 32, 1))
    return x * cos + partner * sin


def _apply_epilogue(kind, acc, aux, j):
    tn = acc.shape[1]
    if kind == "none":
        return acc
    if kind == "relu2":
        r = jnp.maximum(acc, 0.0)
        return r * r
    if kind == "silu":
        return acc * jax.nn.sigmoid(acc)
    if kind == "fgate":
        lb = aux[0][...]
        return lb + (1.0 - lb) * jax.nn.sigmoid(acc)
    if kind in ("rope128", "rope64"):
        cos = aux[0][...]
        sin = aux[1][...]
        fn = _rope128 if kind == "rope128" else _rope64
        parts = [fn(acc[:, c * LANES:(c + 1) * LANES], cos, sin) for c in range(tn // LANES)]
        return jnp.concatenate(parts, axis=1)
    if kind == "qkv":
        cos = aux[0][...]
        sin = aux[1][...]
        n_rope = aux[2]
        parts = []
        for c in range(tn // LANES):
            x = acc[:, c * LANES:(c + 1) * LANES]
            parts.append(jnp.where(j * (tn // LANES) + c < n_rope, _rope128(x, cos, sin), x))
        return jnp.concatenate(parts, axis=1)
    if kind == "idx_tail":
        cos, sin, g, b, wscale = aux[0][...], aux[1][...], aux[2][...], aux[3][...], aux[4]
        lane = lax.broadcasted_iota(jnp.int32, acc.shape, 1)
        is_k = lane < IDX_HEAD_DIM
        mu = jnp.sum(jnp.where(is_k, acc, 0.0), axis=1, keepdims=True) * (1.0 / IDX_HEAD_DIM)
        d = jnp.where(is_k, acc - mu, 0.0)
        var = jnp.sum(d * d, axis=1, keepdims=True) * (1.0 / IDX_HEAD_DIM)
        y = d * lax.rsqrt(var + LN_EPS) * g + b
        y = _rope64(y, cos, sin)
        return jnp.where(is_k, y, acc * wscale)
    raise ValueError(kind)


def _mm_kernel(*refs, nk, kind, n_aux_refs, static_aux):
    a_ref, w_ref = refs[0], refs[1]
    aux_refs = list(refs[2:2 + n_aux_refs]) + list(static_aux)
    o_ref = refs[2 + n_aux_refs]
    j = pl.program_id(1)
    if nk == 1:
        acc = jnp.dot(a_ref[...], w_ref[...], preferred_element_type=F32)
        o_ref[...] = _apply_epilogue(kind, acc, aux_refs, j).astype(o_ref.dtype)
        return
    acc_ref = refs[3 + n_aux_refs]
    k = pl.program_id(2)

    @pl.when(k == 0)
    def _():
        acc_ref[...] = jnp.zeros_like(acc_ref)

    acc_ref[...] += jnp.dot(a_ref[...], w_ref[...], preferred_element_type=F32)

    @pl.when(k == nk - 1)
    def _():
        o_ref[...] = _apply_epilogue(kind, acc_ref[...], aux_refs, j).astype(o_ref.dtype)


def _matmul(a, w, *, kind="none", out_dtype=F32, row_aux=(), col_aux=(), static_aux=(),
            seq=None, cols=None, tm=1024, tn=1024, tk=4096):
    M, K = a.shape
    layer = None
    if isinstance(w, tuple):
        w, layer = w
    lo, hi = (0, w.shape[-1]) if cols is None else cols
    N = hi - lo
    tm = _pick(M if seq is None else seq, tm)
    tn = _pick(math.gcd(N, lo) if lo else N, tn)
    tk = _pick(K, tk)
    nk = K // tk
    j0 = lo // tn
    grid = (M // tm, N // tn, nk)
    if layer is None:
        w_spec = pl.BlockSpec((tk, tn), lambda i, j, k: (k, j + j0))
    else:
        w_spec = pl.BlockSpec((None, tk, tn), lambda i, j, k: (layer, k, j + j0))
    in_specs = [pl.BlockSpec((tm, tk), lambda i, j, k: (i, k)), w_spec]
    args = [a, w]
    for t in row_aux:
        nrow = seq // tm
        in_specs.append(pl.BlockSpec((tm, t.shape[1]), lambda i, j, k, nrow=nrow: (i % nrow, 0)))
        args.append(t)
    for t in col_aux:
        in_specs.append(pl.BlockSpec((1, tn), lambda i, j, k: (0, j)))
        args.append(t)
    scratch = [] if nk == 1 else [pltpu.VMEM((tm, tn), F32)]
    kern = functools.partial(_mm_kernel, nk=nk, kind=kind, n_aux_refs=len(row_aux) + len(col_aux),
                             static_aux=tuple(static_aux))
    return pl.pallas_call(
        kern,
        grid=grid,
        in_specs=in_specs,
        out_specs=pl.BlockSpec((tm, tn), lambda i, j, k: (i, j)),
        out_shape=jax.ShapeDtypeStruct((M, N), out_dtype),
        scratch_shapes=scratch,
        compiler_params=_compiler_params(("parallel", "parallel", "arbitrary")),
        name="mm_" + kind,
    )(*args)


def _ln_kernel(x_ref, h_ref, g_ref, b_ref, o_ref, ob_ref, *, alpha):
    y = alpha * x_ref[...] + h_ref[...]
    mu = jnp.mean(y, axis=-1, keepdims=True)
    d = y - mu
    var = jnp.mean(d * d, axis=-1, keepdims=True)
    out = d * lax.rsqrt(var + LN_EPS) * g_ref[...] + b_ref[...]
    o_ref[...] = out
    ob_ref[...] = out.astype(BF16)


def _residual_ln(x, h, g, b, alpha, tm=256):
    M, D = x.shape
    tm = _pick(M, tm)
    row = pl.BlockSpec((tm, D), lambda i: (i, 0))
    vec = pl.BlockSpec((1, D), lambda i: (0, 0))
    return pl.pallas_call(
        functools.partial(_ln_kernel, alpha=alpha),
        grid=(M // tm,),
        in_specs=[row, row, vec, vec],
        out_specs=[row, row],
        out_shape=[jax.ShapeDtypeStruct((M, D), F32), jax.ShapeDtypeStruct((M, D), BF16)],
        compiler_params=_compiler_params(("parallel",)),
        name="residual_ln",
    )(x, h, g.reshape(1, D), b.reshape(1, D))


PAIR = 2


COUNT_ROWS = 128


def _fori_unrolled(n, body, init, unroll):
    def many(i, carry):
        for u in range(unroll):
            carry = body(unroll * i + u, carry)
        return carry

    carry = lax.fori_loop(0, n // unroll, many, init)
    return lax.fori_loop((n // unroll) * unroll, n, body, carry)


def _for_tiles(n, body, unroll=4):
    def many(i, carry):
        for u in range(unroll):
            body(unroll * i + u)
        return carry

    lax.fori_loop(0, n // unroll, many, 0)
    width = unroll // 2
    while width >= 1:
        start = (n // (2 * width)) * (2 * width)

        @pl.when(n % (2 * width) >= width)
        def _(start=start, width=width):
            for u in range(width):
                body(start + u)

        width //= 2


def _dsa_kernel(q_ref, k_ref, v_ref, qi_ref, wi_ref, kk_ref, o_ref,
                key_ref, stash_ref, slab_ref, mp_ref, acc_ref, *, TQ, TS, NI, topk, idx_bits):
    qb = pl.program_id(1)
    g = pl.program_id(2)
    t0 = qb * TQ
    n_kt = (t0 + TQ + TS - 1) // TS
    reps = TS // LANES
    nt_dims = (((1,), (1,)), ((), ()))

    t_idx = t0 + lax.broadcasted_iota(jnp.int32, (TQ, 1), 0)
    chunk_end = (t_idx // CHUNK + 1) * CHUNK - 1

    def admissible(kt):
        s_idx = kt * TS + lax.broadcasted_iota(jnp.int32, (1, TS), 1)
        return s_idx <= chunk_end

    def wexp_at(h):
        return (h // (PAIR * reps), pl.ds(((h // reps) % PAIR) * TQ, TQ), pl.ds((h % reps) * LANES, LANES))

    @pl.when(g == 0)
    def _select():
        w_all = wi_ref[0] * (IDX_HEAD_DIM ** -0.5)
        lane = lax.broadcasted_iota(jnp.int32, (TQ, LANES), 1)
        for h in range(NI):
            stash_ref[wexp_at(h)] = jnp.broadcast_to(w_all[:, h:h + 1], (TQ, LANES))
            slab = qi_ref[0, :, (h // 2) * LANES:(h // 2 + 1) * LANES]
            keep = (lane < IDX_HEAD_DIM) if h % 2 == 0 else (lane >= IDX_HEAD_DIM)
            slab_ref[h] = jnp.where(keep, slab, jnp.zeros_like(slab))

        def score_tile(kt, carry):
            s0 = pl.multiple_of(kt * TS, TS)
            kk = kk_ref[0, pl.ds(s0, TS), :]
            acc = jnp.zeros((TQ, TS), F32)
            for h in range(NI):
                lg = lax.dot_general(slab_ref[h], kk, nt_dims, preferred_element_type=F32)
                w = jnp.concatenate([stash_ref[wexp_at(h)]] * reps, axis=1)
                acc = acc + w * jnp.maximum(lg, 0.0)
            sc = jnp.where(admissible(kt), acc, -jnp.inf)
            bits = pltpu.bitcast(sc, jnp.int32)
            key_ref[kt] = bits ^ ((bits >> 31) & jnp.int32(0x7FFFFFFF))
            return carry

        lax.fori_loop(0, n_kt, score_tile, 0)

        def count_keys(pred):
            lane_idx = lax.broadcasted_iota(jnp.int32, (COUNT_ROWS, LANES), 1)
            counts = []
            for rb in range(TQ // COUNT_ROWS):
                rows = pl.ds(rb * COUNT_ROWS, COUNT_ROWS)

                def count_tile(kt, c, rows=rows, rb=rb):
                    keys = key_ref[kt, rows, :]
                    for r in range(reps):
                        s_idx = kt * TS + r * LANES + lane_idx
                        c = c + pred(keys[:, r * LANES:(r + 1) * LANES], s_idx, rb).astype(jnp.int32)
                    return c

                counts.append(_fori_unrolled(n_kt, count_tile, jnp.zeros((COUNT_ROWS, LANES), jnp.int32), 2))
            return jnp.sum(jnp.concatenate(counts, axis=0), axis=1, keepdims=True)

        def rows_of(x, rb):
            return x[rb * COUNT_ROWS:(rb + 1) * COUNT_ROWS]

        def bit_step(i, carry):
            thr, n_ge = carry
            cand = thr ^ (jnp.int32(1) << (31 - i))
            tot = count_keys(lambda keys, s_idx, rb: keys >= rows_of(cand, rb))
            ok = tot >= topk
            return jnp.where(ok, cand, thr), jnp.where(ok, tot, n_ge)

        thr, n_ge = lax.fori_loop(0, 32, bit_step, (jnp.full((TQ, LANES), INT32_MIN, jnp.int32),
                                                    jnp.full((TQ, 1), n_kt * TS, jnp.int32)))
        thr_w = jnp.concatenate([thr] * reps, axis=1)
        surplus_ties = jnp.max(n_ge) > topk

        def write_bias(select):
            def bias_tile(kt, carry):
                bias = jnp.where(select(key_ref[kt], kt) & admissible(kt), 0.0, -jnp.inf).astype(F32)
                key_ref[kt] = pltpu.bitcast(bias, jnp.int32)
                return carry

            lax.fori_loop(0, n_kt, bias_tile, 0)

        @pl.when(jnp.logical_not(surplus_ties))
        def _():
            write_bias(lambda keys, kt: keys >= thr_w)

        @pl.when(surplus_ties)
        def _():
            n_gt = count_keys(lambda keys, s_idx, rb: keys > rows_of(thr, rb))
            need = topk - n_gt

            def idx_step(i, last):
                cand = last | (jnp.int32(1) << (idx_bits - 1 - i))
                below = count_keys(
                    lambda keys, s_idx, rb: (keys == rows_of(thr, rb)) & (s_idx < rows_of(cand, rb)))
                return jnp.where(below < need, cand, last)

            last = lax.fori_loop(0, idx_bits, idx_step, jnp.zeros((TQ, LANES), jnp.int32))
            last_w = jnp.concatenate([last] * reps, axis=1)

            def select(keys, kt):
                s_idx = kt * TS + lax.broadcasted_iota(jnp.int32, (TQ, TS), 1)
                return (keys > thr_w) | ((keys == thr_w) & (s_idx <= last_w))

            write_bias(select)

    scale = HEAD_DIM ** -0.5
    ones_blk = jnp.ones((TS, HEAD_DIM), BF16)
    for pr in range(GROUP // PAIR):
        heads = range(pr * PAIR, (pr + 1) * PAIR)
        qs = jnp.concatenate([q_ref[0, :, c * HEAD_DIM:(c + 1) * HEAD_DIM] for c in heads], axis=0)
        mp_ref[...] = jnp.full_like(mp_ref, -jnp.inf)

        def logits_tile(kt):
            s0 = pl.multiple_of(kt * TS, TS)
            s = lax.dot_general(qs, k_ref[0, pl.ds(s0, TS), :], nt_dims, preferred_element_type=F32) * scale
            bias = pltpu.bitcast(key_ref[kt], F32)
            s = s + jnp.concatenate([bias] * PAIR, axis=0)
            stash_ref[kt] = s
            mp = mp_ref[...]
            for r in range(reps):
                mp = jnp.maximum(mp, s[:, r * LANES:(r + 1) * LANES])
            mp_ref[...] = mp

        _for_tiles(n_kt, logits_tile)
        m = jnp.max(mp_ref[...], axis=1, keepdims=True)
        m = jnp.where(m == -jnp.inf, 0.0, m)
        m_w = jnp.broadcast_to(m, (PAIR * TQ, TS))
        acc_ref[...] = jnp.zeros_like(acc_ref)

        def pv_tile(kt):
            s0 = pl.multiple_of(kt * TS, TS)
            p = jnp.exp(stash_ref[kt] - m_w).astype(BF16)
            v1 = jnp.concatenate([v_ref[0, pl.ds(s0, TS), :], ones_blk], axis=1)
            acc_ref[...] += jnp.dot(p, v1, preferred_element_type=F32)

        _for_tiles(n_kt, pv_tile)
        acc = acc_ref[...]
        o = acc[:, :HEAD_DIM] / acc[:, HEAD_DIM:]
        for i, c in enumerate(heads):
            o_ref[0, :, c * HEAD_DIM:(c + 1) * HEAD_DIM] = o[i * TQ:(i + 1) * TQ].astype(o_ref.dtype)


def _dsa_attention(qkv, qi, wi, kk, *, n_heads, topk, TQ=256, TS=512):
    B, S, _ = qkv.shape
    n_kv = n_heads // GROUP
    NI = wi.shape[-1]
    TQ = _pick(S, TQ)
    TS = _pick(S, TS)
    n_tiles = S // TS
    assert NI <= n_tiles * PAIR * (TS // LANES), "head-weight table does not fit in the logits stash"
    kern = functools.partial(_dsa_kernel, TQ=TQ, TS=TS, NI=NI, topk=topk, idx_bits=(S - 1).bit_length())
    gw = GROUP * HEAD_DIM
    return pl.pallas_call(
        kern,
        grid=(B, S // TQ, n_kv),
        in_specs=[
            pl.BlockSpec((1, TQ, gw), lambda b, i, g: (b, i, g)),
            pl.BlockSpec((1, S, HEAD_DIM), lambda b, i, g: (b, 0, n_heads + g)),
            pl.BlockSpec((1, S, HEAD_DIM), lambda b, i, g: (b, 0, n_heads + n_kv + g)),
            pl.BlockSpec((1, TQ, NI * IDX_HEAD_DIM), lambda b, i, g: (b, i, 0)),
            pl.BlockSpec((1, TQ, NI), lambda b, i, g: (b, i, 0)),
            pl.BlockSpec((1, S, LANES), lambda b, i, g: (b, 0, 0)),
        ],
        out_specs=pl.BlockSpec((1, TQ, gw), lambda b, i, g: (b, i, g)),
        out_shape=jax.ShapeDtypeStruct((B, S, n_heads * HEAD_DIM), BF16),
        scratch_shapes=[
            pltpu.VMEM((n_tiles, TQ, TS), jnp.int32),
            pltpu.VMEM((n_tiles, PAIR * TQ, TS), F32),
            pltpu.VMEM((NI, TQ, LANES), BF16),
            pltpu.VMEM((PAIR * TQ, LANES), F32),
            pltpu.VMEM((PAIR * TQ, 2 * HEAD_DIM), F32),
        ],
        compiler_params=_compiler_params(("parallel", "parallel", "arbitrary")),
        name="dsa_attention",
    )(qkv, qkv, qkv, qi, wi, kk)


SUB = 16


def _hgrn_kernel(q_ref, f_ref, v_ref, gate_ref, ng_ref, o_ref, state_ref, *, TT, HG):
    @pl.when(pl.program_id(2) == 0)
    def _():
        state_ref[...] = jnp.zeros_like(state_ref)

    C = CHUNK
    row = lax.broadcasted_iota(jnp.int32, (C, C), 0)
    col = lax.broadcasted_iota(jnp.int32, (C, C), 1)
    tril = (row >= col).astype(F32)
    n_sub = C // SUB
    sub_row = lax.broadcasted_iota(jnp.int32, (SUB, C), 0)
    sub_col = lax.broadcasted_iota(jnp.int32, (SUB, C), 1)
    nt_dims = (((1,), (1,)), ((), ()))
    ng = ng_ref[...]

    def chunk_body(c, carry):
        r0 = pl.multiple_of(c * C, C)
        for h in range(HG):
            cols = slice(h * HGRN_EXPAND, (h + 1) * HGRN_EXPAND)
            q = q_ref[0, pl.ds(r0, C), cols]
            f = f_ref[0, pl.ds(r0, C), cols]
            v = v_ref[0, pl.ds(r0, C), cols]
            gate = gate_ref[0, pl.ds(r0, C), cols]
            lf = jnp.log(f)
            kin = 1.0 - f
            b = jnp.dot(tril, lf, preferred_element_type=F32, precision=lax.Precision.HIGHEST)
            v16 = v.astype(BF16)
            st = state_ref[h]
            o = lax.dot_general((q * jnp.exp(b)).astype(BF16), st.astype(BF16), nt_dims,
                                preferred_element_type=F32)
            qparts, kparts = [], []
            for i in range(1, n_sub):
                lo, hi = i * SUB, (i + 1) * SUB
                ref = b[lo - 1:lo, :]
                q_i = q[lo:hi] * jnp.exp(b[lo:hi] - ref)
                k_i = kin[:lo] * jnp.exp(ref - b[:lo])
                tail = [jnp.zeros((C - hi, HGRN_EXPAND), F32)] if hi < C else []
                qparts.append(jnp.concatenate([jnp.zeros((lo, HGRN_EXPAND), F32), q_i] + tail, axis=0))
                kparts.append(jnp.concatenate([k_i, jnp.zeros((C - lo, HGRN_EXPAND), F32)], axis=0))
            qcat = jnp.concatenate(qparts, axis=1).astype(BF16)
            kcat = jnp.concatenate(kparts, axis=1).astype(BF16)
            a_off = lax.dot_general(qcat, kcat, nt_dims, preferred_element_type=F32)
            dparts = []
            for i in range(n_sub):
                blk = slice(i * SUB, (i + 1) * SUB)
                bt, qt, kt_ = b[blk], q[blk], kin[blk]
                a = jnp.zeros((SUB, C), F32)
                for s in range(SUB):
                    e = jnp.exp(bt - bt[s:s + 1, :])
                    a_col = jnp.sum(qt * kt_[s:s + 1, :] * e, axis=-1, keepdims=True)
                    a = jnp.where((sub_col == i * SUB + s) & (sub_row >= s), a_col, a)
                dparts.append(a)
            a_all = a_off + jnp.concatenate(dparts, axis=0)
            o = o + jnp.dot(a_all.astype(BF16), v16, preferred_element_type=F32)
            b_last = b[C - 1:C, :]
            kd = (kin * jnp.exp(b_last - b)).astype(BF16)
            upd = lax.dot_general(v16, kd, (((0,), (0,)), ((), ())), preferred_element_type=F32)
            state_ref[h] = jnp.exp(b_last) * st + upd
            ms = jnp.mean(o * o, axis=-1, keepdims=True)
            o_ref[0, pl.ds(r0, C), cols] = (o * lax.rsqrt(ms + RMS_EPS) * ng * gate).astype(o_ref.dtype)
        return carry

    def two_chunks(i, carry):
        chunk_body(2 * i, carry)
        chunk_body(2 * i + 1, carry)
        return carry

    lax.fori_loop(0, TT // (2 * C), two_chunks, 0)


def _hgrn_scan(q, f, v, gate, norm_g, *, TT=1024, HG=4):
    B, S, D = q.shape
    H = D // HGRN_EXPAND
    TT = _pick(S, TT)
    HG = _pick(H, HG)
    blk = pl.BlockSpec((1, TT, HG * HGRN_EXPAND), lambda b, h, t: (b, t, h))
    return pl.pallas_call(
        functools.partial(_hgrn_kernel, TT=TT, HG=HG),
        grid=(B, H // HG, S // TT),
        in_specs=[blk, blk, blk, blk, pl.BlockSpec((1, HGRN_EXPAND), lambda b, h, t: (0, 0))],
        out_specs=blk,
        out_shape=jax.ShapeDtypeStruct((B, S, D), BF16),
        scratch_shapes=[pltpu.VMEM((HG, HGRN_EXPAND, HGRN_EXPAND), F32)],
        compiler_params=_compiler_params(("parallel", "parallel", "arbitrary")),
        name="hgrn_scan",
    )(q, f, v, gate, norm_g.reshape(1, HGRN_EXPAND))


def _rope_tables(seq, dim):
    inv = 1.0 / (ROPE_THETA ** (jnp.arange(0, dim, 2, dtype=F32) / dim))
    ang = jnp.arange(seq, dtype=F32)[:, None] * inv[None, :]
    cos, sin = jnp.cos(ang), jnp.sin(ang)
    reps = LANES // dim
    cos_full = jnp.tile(jnp.concatenate([cos, cos], axis=-1), (1, reps))
    sin_signed = jnp.tile(jnp.concatenate([-sin, sin], axis=-1), (1, reps))
    return cos_full, sin_signed


def _dsa_mixer(xb, B, S, w_in, w_out, kln_g, kln_b):
    D = xb.shape[1]
    n_heads = D // HEAD_DIM
    n_kv = n_heads // GROUP
    d_qkv = (n_heads + 2 * n_kv) * HEAD_DIM
    NI = (w_in[0].shape[-1] - d_qkv - IDX_HEAD_DIM) // (IDX_HEAD_DIM + 1)
    d_qi = NI * IDX_HEAD_DIM
    topk = min(TOPK_MAX, S // 4)
    cos128, sin128 = _rope_tables(S, HEAD_DIM)
    cos64, sin64 = _rope_tables(S, IDX_HEAD_DIM)

    qkv = _matmul(xb, w_in, cols=(0, d_qkv), kind="qkv", out_dtype=BF16, row_aux=(cos128, sin128),
                  static_aux=(n_heads + n_kv,), seq=S)
    qi = _matmul(xb, w_in, cols=(d_qkv, d_qkv + d_qi), kind="rope64", out_dtype=BF16,
                 row_aux=(cos64, sin64), seq=S)
    w_tail_src = w_in[0][w_in[1], :, d_qkv + d_qi:]
    w_wi = w_tail_src[:, :NI]
    w_ki = w_tail_src[:, NI:]
    pad = LANES - IDX_HEAD_DIM - NI
    w_tail = jnp.concatenate([w_ki, w_wi, jnp.zeros((D, pad), BF16)], axis=1)
    g_pad = jnp.pad(kln_g.astype(F32), (0, LANES - IDX_HEAD_DIM)).reshape(1, LANES)
    b_pad = jnp.pad(kln_b.astype(F32), (0, LANES - IDX_HEAD_DIM)).reshape(1, LANES)
    tail = _matmul(xb, w_tail, kind="idx_tail", out_dtype=F32, row_aux=(cos64, sin64),
                   col_aux=(g_pad, b_pad), static_aux=(NI ** -0.5,), seq=S)
    ki = tail[:, :IDX_HEAD_DIM].astype(BF16).reshape(B, S, IDX_HEAD_DIM)
    wi = tail[:, IDX_HEAD_DIM:IDX_HEAD_DIM + NI].reshape(B, S, NI)
    kk = jnp.concatenate([ki, ki], axis=-1)

    o = _dsa_attention(qkv.reshape(B, S, d_qkv), qi.reshape(B, S, d_qi), wi, kk, n_heads=n_heads, topk=topk)
    return _matmul(o.reshape(B * S, D), w_out)


def _hgrn_mixer(xb, B, S, w_in, lb, norm_g, w_out):
    D = xb.shape[1]
    q = _matmul(xb, w_in, cols=(0, D), kind="silu")
    f = _matmul(xb, w_in, cols=(D, 2 * D), kind="fgate", col_aux=(lb.reshape(1, D),))
    v = _matmul(xb, w_in, cols=(2 * D, 3 * D))
    gate = _matmul(xb, w_in, cols=(3 * D, 4 * D), kind="silu")
    shp = (B, S, D)
    o = _hgrn_scan(q.reshape(shp), f.reshape(shp), v.reshape(shp), gate.reshape(shp), norm_g)
    return _matmul(o.reshape(B * S, D), w_out)


def _mlp(xb, w_up, w_down):
    h = _matmul(xb, w_up, kind="relu2", out_dtype=BF16)
    return _matmul(h, w_down)


def kernel(x, attn_w_in, attn_w_out, idx_k_ln_g, idx_k_ln_b, hgrn_w_in, hgrn_lower_bounds, hgrn_norm_g,
           hgrn_w_out, mix_ln_g, mix_ln_b, mlp_w_up, mlp_w_down, mlp_ln_g, mlp_ln_b):
    B, S, D = x.shape
    depth = mix_ln_g.shape[0]
    alpha = (2.0 * depth) ** 0.25
    lb_all = jnp.cumsum(jax.nn.softmax(hgrn_lower_bounds.astype(F32), axis=0), axis=0)
    lb_all = lb_all - lb_all[0:1]
    attn_in16, attn_out16 = attn_w_in.astype(BF16), attn_w_out.astype(BF16)
    hgrn_in16, hgrn_out16 = hgrn_w_in.astype(BF16), hgrn_w_out.astype(BF16)
    up16, down16 = mlp_w_up.astype(BF16), mlp_w_down.astype(BF16)
    xf = x.reshape(B * S, D)
    xb = xf.astype(BF16)
    for layer in range(depth):
        j = layer // 2
        if layer % 2 == 0:
            h = _dsa_mixer(xb, B, S, (attn_in16, j), (attn_out16, j), idx_k_ln_g[j], idx_k_ln_b[j])
        else:
            h = _hgrn_mixer(xb, B, S, (hgrn_in16, j), lb_all[layer], hgrn_norm_g[j], (hgrn_out16, j))
        xf, xb = _residual_ln(xf, h, mix_ln_g[layer], mix_ln_b[layer], alpha)
        h = _mlp(xb, (up16, layer), (down16, layer))
        xf, xb = _residual_ln(xf, h, mlp_ln_g[layer], mlp_ln_b[layer], alpha)
    return xf.reshape(B, S, D)
```

```python
import functools
import math

import jax
import jax.numpy as jnp
from jax import lax
from jax.experimental import pallas as pl
from jax.experimental.pallas import tpu as pltpu

HEAD_DIM = 128
GROUP = 4
IDX_HEAD_DIM = 64
TOPK_MAX = 256
CHUNK = 64
ROPE_THETA = 10000.0
HGRN_EXPAND = 128
LN_EPS = 1e-5
RMS_EPS = 1e-6

V7X_VMEM_BYTES = 64 * 1024 * 1024
VMEM_LIMIT_BYTES = V7X_VMEM_BYTES * 7 // 8
LANES = 128

F32 = jnp.float32
BF16 = jnp.bfloat16
INT32_MIN = -(2 ** 31)


def _compiler_params(semantics):
    return pltpu.CompilerParams(dimension_semantics=semantics, vmem_limit_bytes=VMEM_LIMIT_BYTES)


def _pick(n, pref):
    t = min(n, pref)
    while n % t:
        t //= 2
    return t


def _rope128(x, cos, sin):
    return x * cos + pltpu.roll(x, 64, 1) * sin


def _rope64(x, cos, sin):
    lane = lax.broadcasted_iota(jnp.int32, x.shape, 1)
    first_half = (lane % 64) < 32
    partner = jnp.where(first_half, pltpu.roll(x, 96, 1), pltpu.roll(x, 32, 1))
    return x * cos + partner * sin


def _apply_epilogue(kind, acc, aux, j):
    tn = acc.shape[1]
    if kind == "none":
        return acc
    if kind == "relu2":
        r = jnp.maximum(acc, 0.0)
        return r * r
    if kind == "silu":
        return acc * jax.nn.sigmoid(acc)
    if kind == "fgate":
        lb = aux[0][...]
        return lb + (1.0 - lb) * jax.nn.sigmoid(acc)
    if kind in ("rope128", "rope64"):
        cos = aux[0][...]
        sin = aux[1][...]
        fn = _rope128 if kind == "rope128" else _rope64
        parts = [fn(acc[:, c * LANES:(c + 1) * LANES], cos, sin) for c in range(tn // LANES)]
        return jnp.concatenate(parts, axis=1)
    if kind == "qkv":
        cos = aux[0][...]
        sin = aux[1][...]
        n_rope = aux[2]
        parts = []
        for c in range(tn // LANES):
            x = acc[:, c * LANES:(c + 1) * LANES]
            parts.append(jnp.where(j * (tn // LANES) + c < n_rope, _rope128(x, cos, sin), x))
        return jnp.concatenate(parts, axis=1)
    if kind == "idx_tail":
        cos, sin, g, b, wscale = aux[0][...], aux[1][...], aux[2][...], aux[3][...], aux[4]
        lane = lax.broadcasted_iota(jnp.int32, acc.shape, 1)
        is_k = lane < IDX_HEAD_DIM
        mu = jnp.sum(jnp.where(is_k, acc, 0.0), axis=1, keepdims=True) * (1.0 / IDX_HEAD_DIM)
        d = jnp.where(is_k, acc - mu, 0.0)
        var = jnp.sum(d * d, axis=1, keepdims=True) * (1.0 / IDX_HEAD_DIM)
        y = d * lax.rsqrt(var + LN_EPS) * g + b
        y = _rope64(y, cos, sin)
        return jnp.where(is_k, y, acc * wscale)
    raise ValueError(kind)


def _mm_kernel(*refs, nk, kind, n_aux_refs, static_aux):
    a_ref, w_ref = refs[0], refs[1]
    aux_refs = list(refs[2:2 + n_aux_refs]) + list(static_aux)
    o_ref = refs[2 + n_aux_refs]
    j = pl.program_id(1)
    if nk == 1:
        acc = jnp.dot(a_ref[...], w_ref[...], preferred_element_type=F32)
        o_ref[...] = _apply_epilogue(kind, acc, aux_refs, j).astype(o_ref.dtype)
        return
    acc_ref = refs[3 + n_aux_refs]
    k = pl.program_id(2)

    @pl.when(k == 0)
    def _():
        acc_ref[...] = jnp.zeros_like(acc_ref)

    acc_ref[...] += jnp.dot(a_ref[...], w_ref[...], preferred_element_type=F32)

    @pl.when(k == nk - 1)
    def _():
        o_ref[...] = _apply_epilogue(kind, acc_ref[...], aux_refs, j).astype(o_ref.dtype)


def _matmul(a, w, *, kind="none", out_dtype=F32, row_aux=(), col_aux=(), static_aux=(),
            seq=None, cols=None, tm=1024, tn=1024, tk=4096):
    M, K = a.shape
    layer = None
    if isinstance(w, tuple):
        w, layer = w
    lo, hi = (0, w.shape[-1]) if cols is None else cols
    N = hi - lo
    tm = _pick(M if seq is None else seq, tm)
    tn = _pick(math.gcd(N, lo) if lo else N, tn)
    tk = _pick(K, tk)
    nk = K // tk
    j0 = lo // tn
    grid = (M // tm, N // tn, nk)
    if layer is None:
        w_spec = pl.BlockSpec((tk, tn), lambda i, j, k: (k, j + j0))
    else:
        w_spec = pl.BlockSpec((None, tk, tn), lambda i, j, k: (layer, k, j + j0))
    in_specs = [pl.BlockSpec((tm, tk), lambda i, j, k: (i, k)), w_spec]
    args = [a, w]
    for t in row_aux:
        nrow = seq // tm
        in_specs.append(pl.BlockSpec((tm, t.shape[1]), lambda i, j, k, nrow=nrow: (i % nrow, 0)))
        args.append(t)
    for t in col_aux:
        in_specs.append(pl.BlockSpec((1, tn), lambda i, j, k: (0, j)))
        args.append(t)
    scratch = [] if nk == 1 else [pltpu.VMEM((tm, tn), F32)]
    kern = functools.partial(_mm_kernel, nk=nk, kind=kind, n_aux_refs=len(row_aux) + len(col_aux),
                             static_aux=tuple(static_aux))
    return pl.pallas_call(
        kern,
        grid=grid,
        in_specs=in_specs,
        out_specs=pl.BlockSpec((tm, tn), lambda i, j, k: (i, j)),
        out_shape=jax.ShapeDtypeStruct((M, N), out_dtype),
        scratch_shapes=scratch,
        compiler_params=_compiler_params(("parallel", "parallel", "arbitrary")),
        name="mm_" + kind,
    )(*args)


def _mm_wres_kernel(*refs, kind, n_aux_refs, static_aux):
    a_ref, w_ref = refs[0], refs[1]
    aux_refs = list(refs[2:2 + n_aux_refs]) + list(static_aux)
    o_ref = refs[2 + n_aux_refs]
    wb_ref = refs[3 + n_aux_refs]
    j = pl.program_id(0)

    @pl.when(pl.program_id(1) == 0)
    def _():
        wb_ref[...] = w_ref[...].astype(BF16)

    acc = jnp.dot(a_ref[...], wb_ref[...], preferred_element_type=F32)
    o_ref[...] = _apply_epilogue(kind, acc, aux_refs, j).astype(o_ref.dtype)


def _matmul_wres(a, w, *, kind="none", out_dtype=F32, row_aux=(), col_aux=(), static_aux=(),
                 seq=None, cols=None, tm=1024, tn=512):
    M, K = a.shape
    w, layer = w
    lo, hi = (0, w.shape[-1]) if cols is None else cols
    N = hi - lo
    tm = _pick(M if seq is None else seq, tm)
    tn = _pick(math.gcd(N, lo) if lo else N, tn)
    j0 = lo // tn
    in_specs = [pl.BlockSpec((tm, K), lambda j, i: (i, 0)),
                pl.BlockSpec((None, K, tn), lambda j, i: (layer, 0, j + j0))]
    args = [a, w]
    for t in row_aux:
        nrow = seq // tm
        in_specs.append(pl.BlockSpec((tm, t.shape[1]), lambda j, i, nrow=nrow: (i % nrow, 0)))
        args.append(t)
    for t in col_aux:
        in_specs.append(pl.BlockSpec((1, tn), lambda j, i: (0, j)))
        args.append(t)
    kern = functools.partial(_mm_wres_kernel, kind=kind, n_aux_refs=len(row_aux) + len(col_aux),
                             static_aux=tuple(static_aux))
    return pl.pallas_call(
        kern,
        grid=(N // tn, M // tm),
        in_specs=in_specs,
        out_specs=pl.BlockSpec((tm, tn), lambda j, i: (i, j)),
        out_shape=jax.ShapeDtypeStruct((M, N), out_dtype),
        scratch_shapes=[pltpu.VMEM((K, tn), BF16)],
        compiler_params=_compiler_params(("arbitrary", "arbitrary")),
        name="mmw_" + kind,
    )(*args)


def _ln_kernel(x_ref, h_ref, g_ref, b_ref, o_ref, ob_ref, *, alpha):
    y = alpha * x_ref[...] + h_ref[...]
    mu = jnp.mean(y, axis=-1, keepdims=True)
    d = y - mu
    var = jnp.mean(d * d, axis=-1, keepdims=True)
    out = d * lax.rsqrt(var + LN_EPS) * g_ref[...] + b_ref[...]
    o_ref[...] = out
    ob_ref[...] = out.astype(BF16)


def _residual_ln(x, h, g, b, alpha, tm=256):
    M, D = x.shape
    tm = _pick(M, tm)
    row = pl.BlockSpec((tm, D), lambda i: (i, 0))
    vec = pl.BlockSpec((1, D), lambda i: (0, 0))
    return pl.pallas_call(
        functools.partial(_ln_kernel, alpha=alpha),
        grid=(M // tm,),
        in_specs=[row, row, vec, vec],
        out_specs=[row, row],
        out_shape=[jax.ShapeDtypeStruct((M, D), F32), jax.ShapeDtypeStruct((M, D), BF16)],
        compiler_params=_compiler_params(("parallel",)),
        name="residual_ln",
    )(x, h, g.reshape(1, D), b.reshape(1, D))


PAIR = 2


COUNT_ROWS = 128


def _fori_unrolled(n, body, init, unroll):
    def many(i, carry):
        for u in range(unroll):
            carry = body(unroll * i + u, carry)
        return carry

    carry = lax.fori_loop(0, n // unroll, many, init)
    return lax.fori_loop((n // unroll) * unroll, n, body, carry)


def _for_tiles(n, body, unroll=4):
    def many(i, carry):
        for u in range(unroll):
            body(unroll * i + u)
        return carry

    lax.fori_loop(0, n // unroll, many, 0)
    width = unroll // 2
    while width >= 1:
        start = (n // (2 * width)) * (2 * width)

        @pl.when(n % (2 * width) >= width)
        def _(start=start, width=width):
            for u in range(width):
                body(start + u)

        width //= 2


def _dsa_kernel(q_ref, k_ref, v_ref, qi_ref, wi_ref, kk_ref, o_ref,
                key_ref, stash_ref, slab_ref, mp_ref, acc_ref, *, TQ, TS, NI, topk, idx_bits):
    qb = pl.program_id(1)
    g = pl.program_id(2)
    t0 = qb * TQ
    n_kt = (t0 + TQ + TS - 1) // TS
    reps = TS // LANES
    nt_dims = (((1,), (1,)), ((), ()))

    t_idx = t0 + lax.broadcasted_iota(jnp.int32, (TQ, 1), 0)
    chunk_end = (t_idx // CHUNK + 1) * CHUNK - 1

    def admissible(kt):
        s_idx = kt * TS + lax.broadcasted_iota(jnp.int32, (1, TS), 1)
        return s_idx <= chunk_end

    def wexp_at(h):
        return (h // (PAIR * reps), pl.ds(((h // reps) % PAIR) * TQ, TQ), pl.ds((h % reps) * LANES, LANES))

    @pl.when(g == 0)
    def _select():
        w_all = wi_ref[0] * (IDX_HEAD_DIM ** -0.5)
        lane = lax.broadcasted_iota(jnp.int32, (TQ, LANES), 1)
        for h in range(NI):
            stash_ref[wexp_at(h)] = jnp.broadcast_to(w_all[:, h:h + 1], (TQ, LANES))
            slab = qi_ref[0, :, (h // 2) * LANES:(h // 2 + 1) * LANES]
            keep = (lane < IDX_HEAD_DIM) if h % 2 == 0 else (lane >= IDX_HEAD_DIM)
            slab_ref[h] = jnp.where(keep, slab, jnp.zeros_like(slab))

        def score_tile(kt, carry):
            s0 = pl.multiple_of(kt * TS, TS)
            kk = kk_ref[0, pl.ds(s0, TS), :]
            acc = jnp.zeros((TQ, TS), F32)
            for h in range(NI):
                lg = lax.dot_general(slab_ref[h], kk, nt_dims, preferred_element_type=F32)
                w = jnp.concatenate([stash_ref[wexp_at(h)]] * reps, axis=1)
                acc = acc + w * jnp.maximum(lg, 0.0)
            sc = jnp.where(admissible(kt), acc, -jnp.inf)
            bits = pltpu.bitcast(sc, jnp.int32)
            key_ref[kt] = bits ^ ((bits >> 31) & jnp.int32(0x7FFFFFFF))
            return carry

        lax.fori_loop(0, n_kt, score_tile, 0)

        def count_keys(pred):
            lane_idx = lax.broadcasted_iota(jnp.int32, (COUNT_ROWS, LANES), 1)
            counts = []
            for rb in range(TQ // COUNT_ROWS):
                rows = pl.ds(rb * COUNT_ROWS, COUNT_ROWS)

                def count_tile(kt, c, rows=rows, rb=rb):
                    keys = key_ref[kt, rows, :]
                    for r in range(reps):
                        s_idx = kt * TS + r * LANES + lane_idx
                        c = c + pred(keys[:, r * LANES:(r + 1) * LANES], s_idx, rb).astype(jnp.int32)
                    return c

                counts.append(_fori_unrolled(n_kt, count_tile, jnp.zeros((COUNT_ROWS, LANES), jnp.int32), 2))
            return jnp.sum(jnp.concatenate(counts, axis=0), axis=1, keepdims=True)

        def rows_of(x, rb):
            return x[rb * COUNT_ROWS:(rb + 1) * COUNT_ROWS]

        def bit_step(i, carry):
            thr, n_ge = carry
            cand = thr ^ (jnp.int32(1) << (31 - i))
            tot = count_keys(lambda keys, s_idx, rb: keys >= rows_of(cand, rb))
            ok = tot >= topk
            return jnp.where(ok, cand, thr), jnp.where(ok, tot, n_ge)

        thr, n_ge = lax.fori_loop(0, 32, bit_step, (jnp.full((TQ, LANES), INT32_MIN, jnp.int32),
                                                    jnp.full((TQ, 1), n_kt * TS, jnp.int32)))
        thr_w = jnp.concatenate([thr] * reps, axis=1)
        surplus_ties = jnp.max(n_ge) > topk

        def write_bias(select):
            def bias_tile(kt, carry):
                bias = jnp.where(select(key_ref[kt], kt) & admissible(kt), 0.0, -jnp.inf).astype(F32)
                key_ref[kt] = pltpu.bitcast(bias, jnp.int32)
                return carry

            lax.fori_loop(0, n_kt, bias_tile, 0)

        @pl.when(jnp.logical_not(surplus_ties))
        def _():
            write_bias(lambda keys, kt: keys >= thr_w)

        @pl.when(surplus_ties)
        def _():
            n_gt = count_keys(lambda keys, s_idx, rb: keys > rows_of(thr, rb))
            need = topk - n_gt

            def idx_step(i, last):
                cand = last | (jnp.int32(1) << (idx_bits - 1 - i))
                below = count_keys(
                    lambda keys, s_idx, rb: (keys == rows_of(thr, rb)) & (s_idx < rows_of(cand, rb)))
                return jnp.where(below < need, cand, last)

            last = lax.fori_loop(0, idx_bits, idx_step, jnp.zeros((TQ, LANES), jnp.int32))
            last_w = jnp.concatenate([last] * reps, axis=1)

            def select(keys, kt):
                s_idx = kt * TS + lax.broadcasted_iota(jnp.int32, (TQ, TS), 1)
                return (keys > thr_w) | ((keys == thr_w) & (s_idx <= last_w))

            write_bias(select)

    scale = HEAD_DIM ** -0.5
    ones_blk = jnp.ones((TS, HEAD_DIM), BF16)
    for pr in range(GROUP // PAIR):
        heads = range(pr * PAIR, (pr + 1) * PAIR)
        qs = jnp.concatenate([q_ref[0, :, c * HEAD_DIM:(c + 1) * HEAD_DIM] for c in heads], axis=0)
        mp_ref[...] = jnp.full_like(mp_ref, -jnp.inf)

        def logits_tile(kt):
            s0 = pl.multiple_of(kt * TS, TS)
            s = lax.dot_general(qs, k_ref[0, pl.ds(s0, TS), :], nt_dims, preferred_element_type=F32) * scale
            bias = pltpu.bitcast(key_ref[kt], F32)
            s = s + jnp.concatenate([bias] * PAIR, axis=0)
            stash_ref[kt] = s
            mp = mp_ref[...]
            for r in range(reps):
                mp = jnp.maximum(mp, s[:, r * LANES:(r + 1) * LANES])
            mp_ref[...] = mp

        _for_tiles(n_kt, logits_tile)
        m = jnp.max(mp_ref[...], axis=1, keepdims=True)
        m = jnp.where(m == -jnp.inf, 0.0, m)
        m_w = jnp.broadcast_to(m, (PAIR * TQ, TS))
        acc_ref[...] = jnp.zeros_like(acc_ref)

        def pv_tile(kt):
            s0 = pl.multiple_of(kt * TS, TS)
            p = jnp.exp(stash_ref[kt] - m_w).astype(BF16)
            v1 = jnp.concatenate([v_ref[0, pl.ds(s0, TS), :], ones_blk], axis=1)
            acc_ref[...] += jnp.dot(p, v1, preferred_element_type=F32)

        _for_tiles(n_kt, pv_tile)
        acc = acc_ref[...]
        o = acc[:, :HEAD_DIM] / acc[:, HEAD_DIM:]
        for i, c in enumerate(heads):
            o_ref[0, :, c * HEAD_DIM:(c + 1) * HEAD_DIM] = o[i * TQ:(i + 1) * TQ].astype(o_ref.dtype)


def _dsa_attention(qkv, qi, wi, kk, *, n_heads, topk, TQ=256, TS=512):
    B, S, _ = qkv.shape
    n_kv = n_heads // GROUP
    NI = wi.shape[-1]
    TQ = _pick(S, TQ)
    TS = _pick(S, TS)
    n_tiles = S // TS
    assert NI <= n_tiles * PAIR * (TS // LANES), "head-weight table does not fit in the logits stash"
    kern = functools.partial(_dsa_kernel, TQ=TQ, TS=TS, NI=NI, topk=topk, idx_bits=(S - 1).bit_length())
    gw = GROUP * HEAD_DIM
    return pl.pallas_call(
        kern,
        grid=(B, S // TQ, n_kv),
        in_specs=[
            pl.BlockSpec((1, TQ, gw), lambda b, i, g: (b, i, g)),
            pl.BlockSpec((1, S, HEAD_DIM), lambda b, i, g: (b, 0, n_heads + g)),
            pl.BlockSpec((1, S, HEAD_DIM), lambda b, i, g: (b, 0, n_heads + n_kv + g)),
            pl.BlockSpec((1, TQ, NI * IDX_HEAD_DIM), lambda b, i, g: (b, i, 0)),
            pl.BlockSpec((1, TQ, NI), lambda b, i, g: (b, i, 0)),
            pl.BlockSpec((1, S, LANES), lambda b, i, g: (b, 0, 0)),
        ],
        out_specs=pl.BlockSpec((1, TQ, gw), lambda b, i, g: (b, i, g)),
        out_shape=jax.ShapeDtypeStruct((B, S, n_heads * HEAD_DIM), BF16),
        scratch_shapes=[
            pltpu.VMEM((n_tiles, TQ, TS), jnp.int32),
            pltpu.VMEM((n_tiles, PAIR * TQ, TS), F32),
            pltpu.VMEM((NI, TQ, LANES), BF16),
            pltpu.VMEM((PAIR * TQ, LANES), F32),
            pltpu.VMEM((PAIR * TQ, 2 * HEAD_DIM), F32),
        ],
        compiler_params=_compiler_params(("parallel", "parallel", "arbitrary")),
        name="dsa_attention",
    )(qkv, qkv, qkv, qi, wi, kk)


SUB = 16


def _hgrn_kernel(q_ref, f_ref, v_ref, gate_ref, ng_ref, o_ref, state_ref, *, TT, HG):
    @pl.when(pl.program_id(2) == 0)
    def _():
        state_ref[...] = jnp.zeros_like(state_ref)

    C = CHUNK
    row = lax.broadcasted_iota(jnp.int32, (C, C), 0)
    col = lax.broadcasted_iota(jnp.int32, (C, C), 1)
    tril = (row >= col).astype(F32)
    n_sub = C // SUB
    sub_row = lax.broadcasted_iota(jnp.int32, (SUB, C), 0)
    sub_col = lax.broadcasted_iota(jnp.int32, (SUB, C), 1)
    nt_dims = (((1,), (1,)), ((), ()))
    ng = ng_ref[...]

    def chunk_body(c, carry):
        r0 = pl.multiple_of(c * C, C)
        for h in range(HG):
            cols = slice(h * HGRN_EXPAND, (h + 1) * HGRN_EXPAND)
            q = q_ref[0, pl.ds(r0, C), cols]
            f = f_ref[0, pl.ds(r0, C), cols]
            v = v_ref[0, pl.ds(r0, C), cols]
            gate = gate_ref[0, pl.ds(r0, C), cols]
            lf = jnp.log(f)
            kin = 1.0 - f
            b = jnp.dot(tril, lf, preferred_element_type=F32, precision=lax.Precision.HIGHEST)
            v16 = v.astype(BF16)
            st = state_ref[h]
            o = lax.dot_general((q * jnp.exp(b)).astype(BF16), st.astype(BF16), nt_dims,
                                preferred_element_type=F32)
            qparts, kparts = [], []
            for i in range(1, n_sub):
                lo, hi = i * SUB, (i + 1) * SUB
                ref = b[lo - 1:lo, :]
                q_i = q[lo:hi] * jnp.exp(b[lo:hi] - ref)
                k_i = kin[:lo] * jnp.exp(ref - b[:lo])
                tail = [jnp.zeros((C - hi, HGRN_EXPAND), F32)] if hi < C else []
                qparts.append(jnp.concatenate([jnp.zeros((lo, HGRN_EXPAND), F32), q_i] + tail, axis=0))
                kparts.append(jnp.concatenate([k_i, jnp.zeros((C - lo, HGRN_EXPAND), F32)], axis=0))
            qcat = jnp.concatenate(qparts, axis=1).astype(BF16)
            kcat = jnp.concatenate(kparts, axis=1).astype(BF16)
            a_off = lax.dot_general(qcat, kcat, nt_dims, preferred_element_type=F32)
            dparts = []
            for i in range(n_sub):
                blk = slice(i * SUB, (i + 1) * SUB)
                bt, qt, kt_ = b[blk], q[blk], kin[blk]
                a = jnp.zeros((SUB, C), F32)
                for s in range(SUB):
                    e = jnp.exp(bt - bt[s:s + 1, :])
                    a_col = jnp.sum(qt * kt_[s:s + 1, :] * e, axis=-1, keepdims=True)
                    a = jnp.where((sub_col == i * SUB + s) & (sub_row >= s), a_col, a)
                dparts.append(a)
            a_all = a_off + jnp.concatenate(dparts, axis=0)
            o = o + jnp.dot(a_all.astype(BF16), v16, preferred_element_type=F32)
            b_last = b[C - 1:C, :]
            kd = (kin * jnp.exp(b_last - b)).astype(BF16)
            upd = lax.dot_general(v16, kd, (((0,), (0,)), ((), ())), preferred_element_type=F32)
            state_ref[h] = jnp.exp(b_last) * st + upd
            ms = jnp.mean(o * o, axis=-1, keepdims=True)
            o_ref[0, pl.ds(r0, C), cols] = (o * lax.rsqrt(ms + RMS_EPS) * ng * gate).astype(o_ref.dtype)
        return carry

    def two_chunks(i, carry):
        chunk_body(2 * i, carry)
        chunk_body(2 * i + 1, carry)
        return carry

    lax.fori_loop(0, TT // (2 * C), two_chunks, 0)


def _hgrn_scan(q, f, v, gate, norm_g, *, TT=512, HG=8):
    B, S, D = q.shape
    H = D // HGRN_EXPAND
    TT = _pick(S, TT)
    HG = _pick(H, HG)
    blk = pl.BlockSpec((1, TT, HG * HGRN_EXPAND), lambda b, h, t: (b, t, h))
    return pl.pallas_call(
        functools.partial(_hgrn_kernel, TT=TT, HG=HG),
        grid=(B, H // HG, S // TT),
        in_specs=[blk, blk, blk, blk, pl.BlockSpec((1, HGRN_EXPAND), lambda b, h, t: (0, 0))],
        out_specs=blk,
        out_shape=jax.ShapeDtypeStruct((B, S, D), BF16),
        scratch_shapes=[pltpu.VMEM((HG, HGRN_EXPAND, HGRN_EXPAND), F32)],
        compiler_params=_compiler_params(("parallel", "parallel", "arbitrary")),
        name="hgrn_scan",
    )(q, f, v, gate, norm_g.reshape(1, HGRN_EXPAND))


def _rope_tables(seq, dim):
    inv = 1.0 / (ROPE_THETA ** (jnp.arange(0, dim, 2, dtype=F32) / dim))
    ang = jnp.arange(seq, dtype=F32)[:, None] * inv[None, :]
    cos, sin = jnp.cos(ang), jnp.sin(ang)
    reps = LANES // dim
    cos_full = jnp.tile(jnp.concatenate([cos, cos], axis=-1), (1, reps))
    sin_signed = jnp.tile(jnp.concatenate([-sin, sin], axis=-1), (1, reps))
    return cos_full, sin_signed


def _dsa_mixer(xb, B, S, w_in, w_out, kln_g, kln_b):
    D = xb.shape[1]
    n_heads = D // HEAD_DIM
    n_kv = n_heads // GROUP
    d_qkv = (n_heads + 2 * n_kv) * HEAD_DIM
    NI = (w_in[0].shape[-1] - d_qkv - IDX_HEAD_DIM) // (IDX_HEAD_DIM + 1)
    d_qi = NI * IDX_HEAD_DIM
    topk = min(TOPK_MAX, S // 4)
    cos128, sin128 = _rope_tables(S, HEAD_DIM)
    cos64, sin64 = _rope_tables(S, IDX_HEAD_DIM)

    qkv = _matmul_wres(xb, w_in, cols=(0, d_qkv), kind="qkv", out_dtype=BF16, row_aux=(cos128, sin128),
                       static_aux=(n_heads + n_kv,), seq=S)
    qi = _matmul_wres(xb, w_in, cols=(d_qkv, d_qkv + d_qi), kind="rope64", out_dtype=BF16,
                      row_aux=(cos64, sin64), seq=S)
    w_tail_src = w_in[0][w_in[1], :, d_qkv + d_qi:].astype(BF16)
    w_wi = w_tail_src[:, :NI]
    w_ki = w_tail_src[:, NI:]
    pad = LANES - IDX_HEAD_DIM - NI
    w_tail = jnp.concatenate([w_ki, w_wi, jnp.zeros((D, pad), BF16)], axis=1)
    g_pad = jnp.pad(kln_g.astype(F32), (0, LANES - IDX_HEAD_DIM)).reshape(1, LANES)
    b_pad = jnp.pad(kln_b.astype(F32), (0, LANES - IDX_HEAD_DIM)).reshape(1, LANES)
    tail = _matmul(xb, w_tail, kind="idx_tail", out_dtype=F32, row_aux=(cos64, sin64),
                   col_aux=(g_pad, b_pad), static_aux=(NI ** -0.5,), seq=S)
    ki = tail[:, :IDX_HEAD_DIM].astype(BF16).reshape(B, S, IDX_HEAD_DIM)
    wi = tail[:, IDX_HEAD_DIM:IDX_HEAD_DIM + NI].reshape(B, S, NI)
    kk = jnp.concatenate([ki, ki], axis=-1)

    o = _dsa_attention(qkv.reshape(B, S, d_qkv), qi.reshape(B, S, d_qi), wi, kk, n_heads=n_heads, topk=topk)
    return _matmul_wres(o.reshape(B * S, D), w_out)


def _hgrn_mixer(xb, B, S, w_in, lb, norm_g, w_out):
    D = xb.shape[1]
    q = _matmul_wres(xb, w_in, cols=(0, D), kind="silu")
    f = _matmul_wres(xb, w_in, cols=(D, 2 * D), kind="fgate", col_aux=(lb.reshape(1, D),))
    v = _matmul_wres(xb, w_in, cols=(2 * D, 3 * D))
    gate = _matmul_wres(xb, w_in, cols=(3 * D, 4 * D), kind="silu")
    shp = (B, S, D)
    o = _hgrn_scan(q.reshape(shp), f.reshape(shp), v.reshape(shp), gate.reshape(shp), norm_g)
    return _matmul_wres(o.reshape(B * S, D), w_out)


def _mlp(xb, w_up, w_down):
    h = _matmul_wres(xb, w_up, kind="relu2", out_dtype=BF16)
    return _matmul(h, w_down)


def kernel(x, attn_w_in, attn_w_out, idx_k_ln_g, idx_k_ln_b, hgrn_w_in, hgrn_lower_bounds, hgrn_norm_g,
           hgrn_w_out, mix_ln_g, mix_ln_b, mlp_w_up, mlp_w_down, mlp_ln_g, mlp_ln_b):
    B, S, D = x.shape
    depth = mix_ln_g.shape[0]
    alpha = (2.0 * depth) ** 0.25
    lb_all = jnp.cumsum(jax.nn.softmax(hgrn_lower_bounds.astype(F32), axis=0), axis=0)
    lb_all = lb_all - lb_all[0:1]
    down16 = mlp_w_down.astype(BF16)
    xf = x.reshape(B * S, D)
    xb = xf.astype(BF16)
    for layer in range(depth):
        j = layer // 2
        if layer % 2 == 0:
            h = _dsa_mixer(xb, B, S, (attn_w_in, j), (attn_w_out, j), idx_k_ln_g[j], idx_k_ln_b[j])
        else:
            h = _hgrn_mixer(xb, B, S, (hgrn_w_in, j), lb_all[layer], hgrn_norm_g[j], (hgrn_w_out, j))
        xf, xb = _residual_ln(xf, h, mix_ln_g[layer], mix_ln_b[layer], alpha)
        h = _mlp(xb, (mlp_w_up, layer), (down16, layer))
        xf, xb = _residual_ln(xf, h, mlp_ln_g[layer], mlp_ln_b[layer], alpha)
    return xf.reshape(B, S, D)
```

```python
import functools
import math

import jax
import jax.numpy as jnp
from jax import lax
from jax.experimental import pallas as pl
from jax.experimental.pallas import tpu as pltpu

HEAD_DIM = 128
GROUP = 4
IDX_HEAD_DIM = 64
TOPK_MAX = 256
CHUNK = 64
ROPE_THETA = 10000.0
HGRN_EXPAND = 128
LN_EPS = 1e-5
RMS_EPS = 1e-6

V7X_VMEM_BYTES = 64 * 1024 * 1024
VMEM_LIMIT_BYTES = V7X_VMEM_BYTES * 7 // 8
LANES = 128

F32 = jnp.float32
BF16 = jnp.bfloat16
INT32_MIN = -(2 ** 31)


def _compiler_params(semantics):
    return pltpu.CompilerParams(dimension_semantics=semantics, vmem_limit_bytes=VMEM_LIMIT_BYTES)


def _pick(n, pref):
    t = min(n, pref)
    while n % t:
        t //= 2
    return t


def _rope128(x, cos, sin):
    return x * cos + pltpu.roll(x, 64, 1) * sin


def _rope64(x, cos, sin):
    lane = lax.broadcasted_iota(jnp.int32, x.shape, 1)
    first_half = (lane % 64) < 32
    partner = jnp.where(first_half, pltpu.roll(x, 96, 1), pltpu.roll(x, 32, 1))
    return x * cos + partner * sin


def _apply_epilogue(kind, acc, aux, j):
    tn = acc.shape[1]
    if kind == "none":
        return acc
    if kind == "relu2":
        r = jnp.maximum(acc, 0.0)
        return r * r
    if kind == "silu":
        return acc * jax.nn.sigmoid(acc)
    if kind == "fgate":
        lb = aux[0][...]
        return lb + (1.0 - lb) * jax.nn.sigmoid(acc)
    if kind in ("rope128", "rope64"):
        cos = aux[0][...]
        sin = aux[1][...]
        fn = _rope128 if kind == "rope128" else _rope64
        parts = [fn(acc[:, c * LANES:(c + 1) * LANES], cos, sin) for c in range(tn // LANES)]
        return jnp.concatenate(parts, axis=1)
    if kind == "qkv":
        cos = aux[0][...]
        sin = aux[1][...]
        n_rope = aux[2]
        parts = []
        for c in range(tn // LANES):
            x = acc[:, c * LANES:(c + 1) * LANES]
            parts.append(jnp.where(j * (tn // LANES) + c < n_rope, _rope128(x, cos, sin), x))
        return jnp.concatenate(parts, axis=1)
    if kind == "idx_tail":
        cos, sin, g, b, wscale = aux[0][...], aux[1][...], aux[2][...], aux[3][...], aux[4]
        lane = lax.broadcasted_iota(jnp.int32, acc.shape, 1)
        is_k = lane < IDX_HEAD_DIM
        mu = jnp.sum(jnp.where(is_k, acc, 0.0), axis=1, keepdims=True) * (1.0 / IDX_HEAD_DIM)
        d = jnp.where(is_k, acc - mu, 0.0)
        var = jnp.sum(d * d, axis=1, keepdims=True) * (1.0 / IDX_HEAD_DIM)
        y = d * lax.rsqrt(var + LN_EPS) * g + b
        y = _rope64(y, cos, sin)
        return jnp.where(is_k, y, acc * wscale)
    raise ValueError(kind)


def _mm_kernel(*refs, nk, kind, n_aux_refs, static_aux):
    a_ref, w_ref = refs[0], refs[1]
    aux_refs = list(refs[2:2 + n_aux_refs]) + list(static_aux)
    o_ref = refs[2 + n_aux_refs]
    j = pl.program_id(1)
    if nk == 1:
        acc = jnp.dot(a_ref[...], w_ref[...], preferred_element_type=F32)
        o_ref[...] = _apply_epilogue(kind, acc, aux_refs, j).astype(o_ref.dtype)
        return
    acc_ref = refs[3 + n_aux_refs]
    k = pl.program_id(2)

    @pl.when(k == 0)
    def _():
        acc_ref[...] = jnp.zeros_like(acc_ref)

    acc_ref[...] += jnp.dot(a_ref[...], w_ref[...], preferred_element_type=F32)

    @pl.when(k == nk - 1)
    def _():
        o_ref[...] = _apply_epilogue(kind, acc_ref[...], aux_refs, j).astype(o_ref.dtype)


def _matmul(a, w, *, kind="none", out_dtype=F32, row_aux=(), col_aux=(), static_aux=(),
            seq=None, cols=None, tm=1024, tn=1024, tk=4096):
    M, K = a.shape
    layer = None
    if isinstance(w, tuple):
        w, layer = w
    lo, hi = (0, w.shape[-1]) if cols is None else cols
    N = hi - lo
    tm = _pick(M if seq is None else seq, tm)
    tn = _pick(math.gcd(N, lo) if lo else N, tn)
    tk = _pick(K, tk)
    nk = K // tk
    j0 = lo // tn
    grid = (M // tm, N // tn, nk)
    if layer is None:
        w_spec = pl.BlockSpec((tk, tn), lambda i, j, k: (k, j + j0))
    else:
        w_spec = pl.BlockSpec((None, tk, tn), lambda i, j, k: (layer, k, j + j0))
    in_specs = [pl.BlockSpec((tm, tk), lambda i, j, k: (i, k)), w_spec]
    args = [a, w]
    for t in row_aux:
        nrow = seq // tm
        in_specs.append(pl.BlockSpec((tm, t.shape[1]), lambda i, j, k, nrow=nrow: (i % nrow, 0)))
        args.append(t)
    for t in col_aux:
        in_specs.append(pl.BlockSpec((1, tn), lambda i, j, k: (0, j)))
        args.append(t)
    scratch = [] if nk == 1 else [pltpu.VMEM((tm, tn), F32)]
    kern = functools.partial(_mm_kernel, nk=nk, kind=kind, n_aux_refs=len(row_aux) + len(col_aux),
                             static_aux=tuple(static_aux))
    return pl.pallas_call(
        kern,
        grid=grid,
        in_specs=in_specs,
        out_specs=pl.BlockSpec((tm, tn), lambda i, j, k: (i, j)),
        out_shape=jax.ShapeDtypeStruct((M, N), out_dtype),
        scratch_shapes=scratch,
        compiler_params=_compiler_params(("parallel", "parallel", "arbitrary")),
        name="mm_" + kind,
    )(*args)


def _ln_kernel(x_ref, h_ref, g_ref, b_ref, o_ref, ob_ref, *, alpha):
    y = alpha * x_ref[...] + h_ref[...]
    mu = jnp.mean(y, axis=-1, keepdims=True)
    d = y - mu
    var = jnp.mean(d * d, axis=-1, keepdims=True)
    out = d * lax.rsqrt(var + LN_EPS) * g_ref[...] + b_ref[...]
    o_ref[...] = out
    ob_ref[...] = out.astype(BF16)


def _residual_ln(x, h, g, b, alpha, tm=256):
    M, D = x.shape
    tm = _pick(M, tm)
    row = pl.BlockSpec((tm, D), lambda i: (i, 0))
    vec = pl.BlockSpec((1, D), lambda i: (0, 0))
    return pl.pallas_call(
        functools.partial(_ln_kernel, alpha=alpha),
        grid=(M // tm,),
        in_specs=[row, row, vec, vec],
        out_specs=[row, row],
        out_shape=[jax.ShapeDtypeStruct((M, D), F32), jax.ShapeDtypeStruct((M, D), BF16)],
        compiler_params=_compiler_params(("parallel",)),
        name="residual_ln",
    )(x, h, g.reshape(1, D), b.reshape(1, D))


PAIR = 2


COUNT_ROWS = 128


def _fori_unrolled(n, body, init, unroll):
    def many(i, carry):
        for u in range(unroll):
            carry = body(unroll * i + u, carry)
        return carry

    carry = lax.fori_loop(0, n // unroll, many, init)
    return lax.fori_loop((n // unroll) * unroll, n, body, carry)


def _for_tiles(n, body, unroll=4):
    def many(i, carry):
        for u in range(unroll):
            body(unroll * i + u)
        return carry

    lax.fori_loop(0, n // unroll, many, 0)
    width = unroll // 2
    while width >= 1:
        start = (n // (2 * width)) * (2 * width)

        @pl.when(n % (2 * width) >= width)
        def _(start=start, width=width):
            for u in range(width):
                body(start + u)

        width //= 2


def _dsa_kernel(q_ref, k_ref, v_ref, qi_ref, wi_ref, kk_ref, o_ref,
                key_ref, hi_ref, lo_ref, stash_ref, slab_ref, mp_ref, acc_ref, *, TQ, TS, NI, topk, idx_bits):
    qb = pl.program_id(1)
    g = pl.program_id(2)
    t0 = qb * TQ
    n_kt = (t0 + TQ + TS - 1) // TS
    reps = TS // LANES
    nt_dims = (((1,), (1,)), ((), ()))

    t_idx = t0 + lax.broadcasted_iota(jnp.int32, (TQ, 1), 0)
    chunk_end = (t_idx // CHUNK + 1) * CHUNK - 1

    def admissible(kt):
        s_idx = kt * TS + lax.broadcasted_iota(jnp.int32, (1, TS), 1)
        return s_idx <= chunk_end

    def wexp_at(h):
        return (h // (PAIR * reps), pl.ds(((h // reps) % PAIR) * TQ, TQ), pl.ds((h % reps) * LANES, LANES))

    @pl.when(g == 0)
    def _select():
        w_all = wi_ref[0] * (IDX_HEAD_DIM ** -0.5)
        lane = lax.broadcasted_iota(jnp.int32, (TQ, LANES), 1)
        for h in range(NI):
            stash_ref[wexp_at(h)] = jnp.broadcast_to(w_all[:, h:h + 1], (TQ, LANES))
            slab = qi_ref[0, :, (h // 2) * LANES:(h // 2 + 1) * LANES]
            keep = (lane < IDX_HEAD_DIM) if h % 2 == 0 else (lane >= IDX_HEAD_DIM)
            slab_ref[h] = jnp.where(keep, slab, jnp.zeros_like(slab))

        def score_tile(kt, carry):
            s0 = pl.multiple_of(kt * TS, TS)
            kk = kk_ref[0, pl.ds(s0, TS), :]
            acc = jnp.zeros((TQ, TS), F32)
            for h in range(NI):
                lg = lax.dot_general(slab_ref[h], kk, nt_dims, preferred_element_type=F32)
                w = jnp.concatenate([stash_ref[wexp_at(h)]] * reps, axis=1)
                acc = acc + w * jnp.maximum(lg, 0.0)
            sc = jnp.where(admissible(kt), acc, -jnp.inf)
            bits = pltpu.bitcast(sc, jnp.int32)
            key = bits ^ ((bits >> 31) & jnp.int32(0x7FFFFFFF))
            key_ref[kt] = key
            hi_ref[kt] = (key >> 16).astype(jnp.int16)
            return carry

        lax.fori_loop(0, n_kt, score_tile, 0)

        def count_keys(pred):
            lane_idx = lax.broadcasted_iota(jnp.int32, (COUNT_ROWS, LANES), 1)
            counts = []
            for rb in range(TQ // COUNT_ROWS):
                rows = pl.ds(rb * COUNT_ROWS, COUNT_ROWS)

                def count_tile(kt, c, rows=rows, rb=rb):
                    keys = key_ref[kt, rows, :]
                    for r in range(reps):
                        s_idx = kt * TS + r * LANES + lane_idx
                        c = c + pred(keys[:, r * LANES:(r + 1) * LANES], s_idx, rb).astype(jnp.int32)
                    return c

                counts.append(_fori_unrolled(n_kt, count_tile, jnp.zeros((COUNT_ROWS, LANES), jnp.int32), 2))
            return jnp.sum(jnp.concatenate(counts, axis=0), axis=1, keepdims=True)

        def rows_of(x, rb):
            return x[rb * COUNT_ROWS:(rb + 1) * COUNT_ROWS]

        def count_plane(plane_ref, pred):
            counts = []
            for rb in range(TQ // COUNT_ROWS):
                rows = pl.ds(rb * COUNT_ROWS, COUNT_ROWS)

                def count_tile(kt, c, rows=rows, rb=rb):
                    vals = plane_ref[kt, rows, :]
                    for r in range(reps):
                        c = c + pred(vals[:, r * LANES:(r + 1) * LANES], rb).astype(jnp.int16)
                    return c

                c = _fori_unrolled(n_kt, count_tile, jnp.zeros((COUNT_ROWS, LANES), jnp.int16), 2)
                counts.append(c.astype(jnp.int32))
            return jnp.sum(jnp.concatenate(counts, axis=0), axis=1, keepdims=True)

        def rows16(x, rb):
            return rows_of(x, rb).astype(jnp.int16)

        def bisect16(plane_ref, need, n_start):
            def step(i, carry):
                thr, n_ge = carry
                cand = thr + (jnp.int32(1) << (15 - i))
                tot = count_plane(plane_ref, lambda vals, rb: vals >= rows16(cand, rb))
                ok = tot >= need
                return jnp.where(ok, cand, thr), jnp.where(ok, tot, n_ge)

            return lax.fori_loop(0, 16, step, (jnp.full((TQ, LANES), -32768, jnp.int32), n_start))

        thr_hi, n_ge_hi = bisect16(hi_ref, topk, jnp.full((TQ, 1), n_kt * TS, jnp.int32))
        n_gt_hi = count_plane(hi_ref, lambda vals, rb: vals > rows16(thr_hi, rb))
        thr_hi_w = jnp.concatenate([thr_hi] * reps, axis=1)

        def lo_tile(kt, carry):
            key = key_ref[kt]
            lo = (key & jnp.int32(0xFFFF)) - 32768
            lo_ref[kt] = jnp.where((key >> 16) == thr_hi_w, lo, -32768).astype(jnp.int16)
            return carry

        lax.fori_loop(0, n_kt, lo_tile, 0)
        thr_lo, n_ge_lo = bisect16(lo_ref, topk - n_gt_hi, n_ge_hi - n_gt_hi)
        thr = (thr_hi << 16) | (thr_lo + 32768)
        n_ge = n_gt_hi + n_ge_lo
        thr_w = jnp.concatenate([thr] * reps, axis=1)
        surplus_ties = jnp.max(n_ge) > topk

        def write_bias(select):
            def bias_tile(kt, carry):
                bias = jnp.where(select(key_ref[kt], kt) & admissible(kt), 0.0, -jnp.inf).astype(F32)
                key_ref[kt] = pltpu.bitcast(bias, jnp.int32)
                return carry

            lax.fori_loop(0, n_kt, bias_tile, 0)

        @pl.when(jnp.logical_not(surplus_ties))
        def _():
            write_bias(lambda keys, kt: keys >= thr_w)

        @pl.when(surplus_ties)
        def _():
            n_gt = count_keys(lambda keys, s_idx, rb: keys > rows_of(thr, rb))
            need = topk - n_gt

            def idx_step(i, last):
                cand = last | (jnp.int32(1) << (idx_bits - 1 - i))
                below = count_keys(
                    lambda keys, s_idx, rb: (keys == rows_of(thr, rb)) & (s_idx < rows_of(cand, rb)))
                return jnp.where(below < need, cand, last)

            last = lax.fori_loop(0, idx_bits, idx_step, jnp.zeros((TQ, LANES), jnp.int32))
            last_w = jnp.concatenate([last] * reps, axis=1)

            def select(keys, kt):
                s_idx = kt * TS + lax.broadcasted_iota(jnp.int32, (TQ, TS), 1)
                return (keys > thr_w) | ((keys == thr_w) & (s_idx <= last_w))

            write_bias(select)

    scale = HEAD_DIM ** -0.5
    ones_blk = jnp.ones((TS, HEAD_DIM), BF16)
    for pr in range(GROUP // PAIR):
        heads = range(pr * PAIR, (pr + 1) * PAIR)
        qs = jnp.concatenate([q_ref[0, :, c * HEAD_DIM:(c + 1) * HEAD_DIM] for c in heads], axis=0)
        mp_ref[...] = jnp.full_like(mp_ref, -jnp.inf)

        def logits_tile(kt):
            s0 = pl.multiple_of(kt * TS, TS)
            s = lax.dot_general(qs, k_ref[0, pl.ds(s0, TS), :], nt_dims, preferred_element_type=F32) * scale
            bias = pltpu.bitcast(key_ref[kt], F32)
            s = s + jnp.concatenate([bias] * PAIR, axis=0)
            stash_ref[kt] = s
            mp = mp_ref[...]
            for r in range(reps):
                mp = jnp.maximum(mp, s[:, r * LANES:(r + 1) * LANES])
            mp_ref[...] = mp

        _for_tiles(n_kt, logits_tile)
        m = jnp.max(mp_ref[...], axis=1, keepdims=True)
        m = jnp.where(m == -jnp.inf, 0.0, m)
        m_w = jnp.broadcast_to(m, (PAIR * TQ, TS))
        acc_ref[...] = jnp.zeros_like(acc_ref)

        def pv_tile(kt):
            s0 = pl.multiple_of(kt * TS, TS)
            p = jnp.exp(stash_ref[kt] - m_w).astype(BF16)
            v1 = jnp.concatenate([v_ref[0, pl.ds(s0, TS), :], ones_blk], axis=1)
            acc_ref[...] += jnp.dot(p, v1, preferred_element_type=F32)

        _for_tiles(n_kt, pv_tile)
        acc = acc_ref[...]
        o = acc[:, :HEAD_DIM] / acc[:, HEAD_DIM:]
        for i, c in enumerate(heads):
            o_ref[0, :, c * HEAD_DIM:(c + 1) * HEAD_DIM] = o[i * TQ:(i + 1) * TQ].astype(o_ref.dtype)


def _dsa_attention(qkv, qi, wi, kk, *, n_heads, topk, TQ=256, TS=512):
    B, S, _ = qkv.shape
    n_kv = n_heads // GROUP
    NI = wi.shape[-1]
    TQ = _pick(S, TQ)
    TS = _pick(S, TS)
    n_tiles = S // TS
    assert NI <= n_tiles * PAIR * (TS // LANES), "head-weight table does not fit in the logits stash"
    kern = functools.partial(_dsa_kernel, TQ=TQ, TS=TS, NI=NI, topk=topk, idx_bits=(S - 1).bit_length())
    gw = GROUP * HEAD_DIM
    return pl.pallas_call(
        kern,
        grid=(B, S // TQ, n_kv),
        in_specs=[
            pl.BlockSpec((1, TQ, gw), lambda b, i, g: (b, i, g)),
            pl.BlockSpec((1, S, HEAD_DIM), lambda b, i, g: (b, 0, n_heads + g)),
            pl.BlockSpec((1, S, HEAD_DIM), lambda b, i, g: (b, 0, n_heads + n_kv + g)),
            pl.BlockSpec((1, TQ, NI * IDX_HEAD_DIM), lambda b, i, g: (b, i, 0)),
            pl.BlockSpec((1, TQ, NI), lambda b, i, g: (b, i, 0)),
            pl.BlockSpec((1, S, LANES), lambda b, i, g: (b, 0, 0)),
        ],
        out_specs=pl.BlockSpec((1, TQ, gw), lambda b, i, g: (b, i, g)),
        out_shape=jax.ShapeDtypeStruct((B, S, n_heads * HEAD_DIM), BF16),
        scratch_shapes=[
            pltpu.VMEM((n_tiles, TQ, TS), jnp.int32),
            pltpu.VMEM((n_tiles, TQ, TS), jnp.int16),
            pltpu.VMEM((n_tiles, TQ, TS), jnp.int16),
            pltpu.VMEM((n_tiles, PAIR * TQ, TS), F32),
            pltpu.VMEM((NI, TQ, LANES), BF16),
            pltpu.VMEM((PAIR * TQ, LANES), F32),
            pltpu.VMEM((PAIR * TQ, 2 * HEAD_DIM), F32),
        ],
        compiler_params=_compiler_params(("parallel", "parallel", "arbitrary")),
        name="dsa_attention",
    )(qkv, qkv, qkv, qi, wi, kk)


SUB = 16


def _hgrn_kernel(q_ref, f_ref, v_ref, gate_ref, ng_ref, o_ref, state_ref, *, TT, HG):
    @pl.when(pl.program_id(2) == 0)
    def _():
        state_ref[...] = jnp.zeros_like(state_ref)

    C = CHUNK
    row = lax.broadcasted_iota(jnp.int32, (C, C), 0)
    col = lax.broadcasted_iota(jnp.int32, (C, C), 1)
    tril = (row >= col).astype(F32)
    n_sub = C // SUB
    sub_row = lax.broadcasted_iota(jnp.int32, (SUB, C), 0)
    sub_col = lax.broadcasted_iota(jnp.int32, (SUB, C), 1)
    nt_dims = (((1,), (1,)), ((), ()))
    ng = ng_ref[...]

    def chunk_body(c, carry):
        r0 = pl.multiple_of(c * C, C)
        for h in range(HG):
            cols = slice(h * HGRN_EXPAND, (h + 1) * HGRN_EXPAND)
            q = q_ref[0, pl.ds(r0, C), cols]
            f = f_ref[0, pl.ds(r0, C), cols]
            v = v_ref[0, pl.ds(r0, C), cols]
            gate = gate_ref[0, pl.ds(r0, C), cols]
            lf = jnp.log(f)
            kin = 1.0 - f
            b = jnp.dot(tril, lf, preferred_element_type=F32, precision=lax.Precision.HIGHEST)
            v16 = v.astype(BF16)
            st = state_ref[h]
            o = lax.dot_general((q * jnp.exp(b)).astype(BF16), st.astype(BF16), nt_dims,
                                preferred_element_type=F32)
            qparts, kparts = [], []
            for i in range(1, n_sub):
                lo, hi = i * SUB, (i + 1) * SUB
                ref = b[lo - 1:lo, :]
                q_i = q[lo:hi] * jnp.exp(b[lo:hi] - ref)
                k_i = kin[:lo] * jnp.exp(ref - b[:lo])
                tail = [jnp.zeros((C - hi, HGRN_EXPAND), F32)] if hi < C else []
                qparts.append(jnp.concatenate([jnp.zeros((lo, HGRN_EXPAND), F32), q_i] + tail, axis=0))
                kparts.append(jnp.concatenate([k_i, jnp.zeros((C - lo, HGRN_EXPAND), F32)], axis=0))
            qcat = jnp.concatenate(qparts, axis=1).astype(BF16)
            kcat = jnp.concatenate(kparts, axis=1).astype(BF16)
            a_off = lax.dot_general(qcat, kcat, nt_dims, preferred_element_type=F32)
            dparts = []
            for i in range(n_sub):
                blk = slice(i * SUB, (i + 1) * SUB)
                bt, qt, kt_ = b[blk], q[blk], kin[blk]
                a = jnp.zeros((SUB, C), F32)
                for s in range(SUB):
                    e = jnp.exp(bt - bt[s:s + 1, :])
                    a_col = jnp.sum(qt * kt_[s:s + 1, :] * e, axis=-1, keepdims=True)
                    a = jnp.where((sub_col == i * SUB + s) & (sub_row >= s), a_col, a)
                dparts.append(a)
            a_all = a_off + jnp.concatenate(dparts, axis=0)
            o = o + jnp.dot(a_all.astype(BF16), v16, preferred_element_type=F32)
            b_last = b[C - 1:C, :]
            kd = (kin * jnp.exp(b_last - b)).astype(BF16)
            upd = lax.dot_general(v16, kd, (((0,), (0,)), ((), ())), preferred_element_type=F32)
            state_ref[h] = jnp.exp(b_last) * st + upd
            ms = jnp.mean(o * o, axis=-1, keepdims=True)
            o_ref[0, pl.ds(r0, C), cols] = (o * lax.rsqrt(ms + RMS_EPS) * ng * gate).astype(o_ref.dtype)
        return carry

    def two_chunks(i, carry):
        chunk_body(2 * i, carry)
        chunk_body(2 * i + 1, carry)
        return carry

    lax.fori_loop(0, TT // (2 * C), two_chunks, 0)


def _hgrn_scan(q, f, v, gate, norm_g, *, TT=512, HG=8):
    B, S, D = q.shape
    H = D // HGRN_EXPAND
    TT = _pick(S, TT)
    HG = _pick(H, HG)
    blk = pl.BlockSpec((1, TT, HG * HGRN_EXPAND), lambda b, h, t: (b, t, h))
    return pl.pallas_call(
        functools.partial(_hgrn_kernel, TT=TT, HG=HG),
        grid=(B, H // HG, S // TT),
        in_specs=[blk, blk, blk, blk, pl.BlockSpec((1, HGRN_EXPAND), lambda b, h, t: (0, 0))],
        out_specs=blk,
        out_shape=jax.ShapeDtypeStruct((B, S, D), BF16),
        scratch_shapes=[pltpu.VMEM((HG, HGRN_EXPAND, HGRN_EXPAND), F32)],
        compiler_params=_compiler_params(("parallel", "parallel", "arbitrary")),
        name="hgrn_scan",
    )(q, f, v, gate, norm_g.reshape(1, HGRN_EXPAND))


def _rope_tables(seq, dim):
    inv = 1.0 / (ROPE_THETA ** (jnp.arange(0, dim, 2, dtype=F32) / dim))
    ang = jnp.arange(seq, dtype=F32)[:, None] * inv[None, :]
    cos, sin = jnp.cos(ang), jnp.sin(ang)
    reps = LANES // dim
    cos_full = jnp.tile(jnp.concatenate([cos, cos], axis=-1), (1, reps))
    sin_signed = jnp.tile(jnp.concatenate([-sin, sin], axis=-1), (1, reps))
    return cos_full, sin_signed


def _dsa_mixer(xb, B, S, w_in, w_out, kln_g, kln_b):
    D = xb.shape[1]
    n_heads = D // HEAD_DIM
    n_kv = n_heads // GROUP
    d_qkv = (n_heads + 2 * n_kv) * HEAD_DIM
    NI = (w_in[0].shape[-1] - d_qkv - IDX_HEAD_DIM) // (IDX_HEAD_DIM + 1)
    d_qi = NI * IDX_HEAD_DIM
    topk = min(TOPK_MAX, S // 4)
    cos128, sin128 = _rope_tables(S, HEAD_DIM)
    cos64, sin64 = _rope_tables(S, IDX_HEAD_DIM)

    qkv = _matmul(xb, w_in, cols=(0, d_qkv), kind="qkv", out_dtype=BF16, row_aux=(cos128, sin128),
                  static_aux=(n_heads + n_kv,), seq=S)
    qi = _matmul(xb, w_in, cols=(d_qkv, d_qkv + d_qi), kind="rope64", out_dtype=BF16,
                 row_aux=(cos64, sin64), seq=S)
    w_tail_src = w_in[0][w_in[1], :, d_qkv + d_qi:]
    w_wi = w_tail_src[:, :NI]
    w_ki = w_tail_src[:, NI:]
    pad = LANES - IDX_HEAD_DIM - NI
    w_tail = jnp.concatenate([w_ki, w_wi, jnp.zeros((D, pad), BF16)], axis=1)
    g_pad = jnp.pad(kln_g.astype(F32), (0, LANES - IDX_HEAD_DIM)).reshape(1, LANES)
    b_pad = jnp.pad(kln_b.astype(F32), (0, LANES - IDX_HEAD_DIM)).reshape(1, LANES)
    tail = _matmul(xb, w_tail, kind="idx_tail", out_dtype=F32, row_aux=(cos64, sin64),
                   col_aux=(g_pad, b_pad), static_aux=(NI ** -0.5,), seq=S)
    ki = tail[:, :IDX_HEAD_DIM].astype(BF16).reshape(B, S, IDX_HEAD_DIM)
    wi = tail[:, IDX_HEAD_DIM:IDX_HEAD_DIM + NI].reshape(B, S, NI)
    kk = jnp.concatenate([ki, ki], axis=-1)

    o = _dsa_attention(qkv.reshape(B, S, d_qkv), qi.reshape(B, S, d_qi), wi, kk, n_heads=n_heads, topk=topk)
    return _matmul(o.reshape(B * S, D), w_out)


def _hgrn_mixer(xb, B, S, w_in, lb, norm_g, w_out):
    D = xb.shape[1]
    q = _matmul(xb, w_in, cols=(0, D), kind="silu")
    f = _matmul(xb, w_in, cols=(D, 2 * D), kind="fgate", col_aux=(lb.reshape(1, D),))
    v = _matmul(xb, w_in, cols=(2 * D, 3 * D))
    gate = _matmul(xb, w_in, cols=(3 * D, 4 * D), kind="silu")
    shp = (B, S, D)
    o = _hgrn_scan(q.reshape(shp), f.reshape(shp), v.reshape(shp), gate.reshape(shp), norm_g)
    return _matmul(o.reshape(B * S, D), w_out)


def _mlp(xb, w_up, w_down):
    h = _matmul(xb, w_up, kind="relu2", out_dtype=BF16)
    return _matmul(h, w_down)


def kernel(x, attn_w_in, attn_w_out, idx_k_ln_g, idx_k_ln_b, hgrn_w_in, hgrn_lower_bounds, hgrn_norm_g,
           hgrn_w_out, mix_ln_g, mix_ln_b, mlp_w_up, mlp_w_down, mlp_ln_g, mlp_ln_b):
    B, S, D = x.shape
    depth = mix_ln_g.shape[0]
    alpha = (2.0 * depth) ** 0.25
    lb_all = jnp.cumsum(jax.nn.softmax(hgrn_lower_bounds.astype(F32), axis=0), axis=0)
    lb_all = lb_all - lb_all[0:1]
    attn_in16, attn_out16 = attn_w_in.astype(BF16), attn_w_out.astype(BF16)
    hgrn_in16, hgrn_out16 = hgrn_w_in.astype(BF16), hgrn_w_out.astype(BF16)
    up16, down16 = mlp_w_up.astype(BF16), mlp_w_down.astype(BF16)
    xf = x.reshape(B * S, D)
    xb = xf.astype(BF16)
    for layer in range(depth):
        j = layer // 2
        if layer % 2 == 0:
            h = _dsa_mixer(xb, B, S, (attn_in16, j), (attn_out16, j), idx_k_ln_g[j], idx_k_ln_b[j])
        else:
            h = _hgrn_mixer(xb, B, S, (hgrn_in16, j), lb_all[layer], hgrn_norm_g[j], (hgrn_out16, j))
        xf, xb = _residual_ln(xf, h, mix_ln_g[layer], mix_ln_b[layer], alpha)
        h = _mlp(xb, (up16, layer), (down16, layer))
        xf, xb = _residual_ln(xf, h, mlp_ln_g[layer], mlp_ln_b[layer], alpha)
    return xf.reshape(B, S, D)
```

```python
import functools
import math

import jax
import jax.numpy as jnp
from jax import lax
from jax.experimental import pallas as pl
from jax.experimental.pallas import tpu as pltpu

HEAD_DIM = 128
GROUP = 4
IDX_HEAD_DIM = 64
TOPK_MAX = 256
CHUNK = 64
ROPE_THETA = 10000.0
HGRN_EXPAND = 128
LN_EPS = 1e-5
RMS_EPS = 1e-6

V7X_VMEM_BYTES = 64 * 1024 * 1024
VMEM_LIMIT_BYTES = V7X_VMEM_BYTES * 7 // 8
LANES = 128

F32 = jnp.float32
BF16 = jnp.bfloat16
INT32_MIN = -(2 ** 31)


def _compiler_params(semantics):
    return pltpu.CompilerParams(dimension_semantics=semantics, vmem_limit_bytes=VMEM_LIMIT_BYTES)


def _pick(n, pref):
    t = min(n, pref)
    while n % t:
        t //= 2
    return t


def _rope128(x, cos, sin):
    return x * cos + pltpu.roll(x, 64, 1) * sin


def _rope64(x, cos, sin):
    lane = lax.broadcasted_iota(jnp.int32, x.shape, 1)
    first_half = (lane % 64) < 32
    partner = jnp.where(first_half, pltpu.roll(x, 96, 1), pltpu.roll(x, 32, 1))
    return x * cos + partner * sin


def _apply_epilogue(kind, acc, aux, j):
    tn = acc.shape[1]
    if kind == "none":
        return acc
    if kind == "relu2":
        r = jnp.maximum(acc, 0.0)
        return r * r
    if kind == "silu":
        return acc * jax.nn.sigmoid(acc)
    if kind == "fgate":
        lb = aux[0][...]
        return lb + (1.0 - lb) * jax.nn.sigmoid(acc)
    if kind in ("rope128", "rope64"):
        cos = aux[0][...]
        sin = aux[1][...]
        fn = _rope128 if kind == "rope128" else _rope64
        parts = [fn(acc[:, c * LANES:(c + 1) * LANES], cos, sin) for c in range(tn // LANES)]
        return jnp.concatenate(parts, axis=1)
    if kind == "qkv":
        cos = aux[0][...]
        sin = aux[1][...]
        n_rope = aux[2]
        parts = []
        for c in range(tn // LANES):
            x = acc[:, c * LANES:(c + 1) * LANES]
            parts.append(jnp.where(j * (tn // LANES) + c < n_rope, _rope128(x, cos, sin), x))
        return jnp.concatenate(parts, axis=1)
    if kind == "idx_tail":
        cos, sin, g, b, wscale = aux[0][...], aux[1][...], aux[2][...], aux[3][...], aux[4]
        lane = lax.broadcasted_iota(jnp.int32, acc.shape, 1)
        is_k = lane < IDX_HEAD_DIM
        mu = jnp.sum(jnp.where(is_k, acc, 0.0), axis=1, keepdims=True) * (1.0 / IDX_HEAD_DIM)
        d = jnp.where(is_k, acc - mu, 0.0)
        var = jnp.sum(d * d, axis=1, keepdims=True) * (1.0 / IDX_HEAD_DIM)
        y = d * lax.rsqrt(var + LN_EPS) * g + b
        y = _rope64(y, cos, sin)
        return jnp.where(is_k, y, acc * wscale)
    raise ValueError(kind)


def _mm_kernel(*refs, nk, kind, n_aux_refs, static_aux):
    a_ref, w_ref = refs[0], refs[1]
    aux_refs = list(refs[2:2 + n_aux_refs]) + list(static_aux)
    o_ref = refs[2 + n_aux_refs]
    j = pl.program_id(1)
    if nk == 1:
        acc = jnp.dot(a_ref[...], w_ref[...], preferred_element_type=F32)
        o_ref[...] = _apply_epilogue(kind, acc, aux_refs, j).astype(o_ref.dtype)
        return
    acc_ref = refs[3 + n_aux_refs]
    k = pl.program_id(2)

    @pl.when(k == 0)
    def _():
        acc_ref[...] = jnp.zeros_like(acc_ref)

    acc_ref[...] += jnp.dot(a_ref[...], w_ref[...], preferred_element_type=F32)

    @pl.when(k == nk - 1)
    def _():
        o_ref[...] = _apply_epilogue(kind, acc_ref[...], aux_refs, j).astype(o_ref.dtype)


def _matmul(a, w, *, kind="none", out_dtype=F32, row_aux=(), col_aux=(), static_aux=(),
            seq=None, cols=None, tm=1024, tn=1024, tk=4096):
    M, K = a.shape
    layer = None
    if isinstance(w, tuple):
        w, layer = w
    lo, hi = (0, w.shape[-1]) if cols is None else cols
    N = hi - lo
    tm = _pick(M if seq is None else seq, tm)
    tn = _pick(math.gcd(N, lo) if lo else N, tn)
    tk = _pick(K, tk)
    nk = K // tk
    j0 = lo // tn
    grid = (M // tm, N // tn, nk)
    if layer is None:
        w_spec = pl.BlockSpec((tk, tn), lambda i, j, k: (k, j + j0))
    else:
        w_spec = pl.BlockSpec((None, tk, tn), lambda i, j, k: (layer, k, j + j0))
    in_specs = [pl.BlockSpec((tm, tk), lambda i, j, k: (i, k)), w_spec]
    args = [a, w]
    for t in row_aux:
        nrow = seq // tm
        in_specs.append(pl.BlockSpec((tm, t.shape[1]), lambda i, j, k, nrow=nrow: (i % nrow, 0)))
        args.append(t)
    for t in col_aux:
        in_specs.append(pl.BlockSpec((1, tn), lambda i, j, k: (0, j)))
        args.append(t)
    scratch = [] if nk == 1 else [pltpu.VMEM((tm, tn), F32)]
    kern = functools.partial(_mm_kernel, nk=nk, kind=kind, n_aux_refs=len(row_aux) + len(col_aux),
                             static_aux=tuple(static_aux))
    return pl.pallas_call(
        kern,
        grid=grid,
        in_specs=in_specs,
        out_specs=pl.BlockSpec((tm, tn), lambda i, j, k: (i, j)),
        out_shape=jax.ShapeDtypeStruct((M, N), out_dtype),
        scratch_shapes=scratch,
        compiler_params=_compiler_params(("parallel", "parallel", "arbitrary")),
        name="mm_" + kind,
    )(*args)


def _ln_kernel(x_ref, h_ref, g_ref, b_ref, o_ref, ob_ref, *, alpha):
    y = alpha * x_ref[...] + h_ref[...]
    mu = jnp.mean(y, axis=-1, keepdims=True)
    d = y - mu
    var = jnp.mean(d * d, axis=-1, keepdims=True)
    out = d * lax.rsqrt(var + LN_EPS) * g_ref[...] + b_ref[...]
    o_ref[...] = out
    ob_ref[...] = out.astype(BF16)


def _residual_ln(x, h, g, b, alpha, tm=256):
    M, D = x.shape
    tm = _pick(M, tm)
    row = pl.BlockSpec((tm, D), lambda i: (i, 0))
    vec = pl.BlockSpec((1, D), lambda i: (0, 0))
    return pl.pallas_call(
        functools.partial(_ln_kernel, alpha=alpha),
        grid=(M // tm,),
        in_specs=[row, row, vec, vec],
        out_specs=[row, row],
        out_shape=[jax.ShapeDtypeStruct((M, D), F32), jax.ShapeDtypeStruct((M, D), BF16)],
        compiler_params=_compiler_params(("parallel",)),
        name="residual_ln",
    )(x, h, g.reshape(1, D), b.reshape(1, D))


PAIR = 2


COUNT_ROWS = 128


def _fori_unrolled(n, body, init, unroll):
    def many(i, carry):
        for u in range(unroll):
            carry = body(unroll * i + u, carry)
        return carry

    carry = lax.fori_loop(0, n // unroll, many, init)
    return lax.fori_loop((n // unroll) * unroll, n, body, carry)


def _for_tiles(n, body, unroll=4):
    def many(i, carry):
        for u in range(unroll):
            body(unroll * i + u)
        return carry

    lax.fori_loop(0, n // unroll, many, 0)
    width = unroll // 2
    while width >= 1:
        start = (n // (2 * width)) * (2 * width)

        @pl.when(n % (2 * width) >= width)
        def _(start=start, width=width):
            for u in range(width):
                body(start + u)

        width //= 2


def _dsa_kernel(q_ref, k_ref, v_ref, qi_ref, wi_ref, kk_ref, o_ref,
                key_ref, stash_ref, slab_ref, mp_ref, acc_ref, *, TQ, TS, NI, topk, idx_bits):
    qb = pl.program_id(1)
    g = pl.program_id(2)
    t0 = qb * TQ
    n_kt = (t0 + TQ + TS - 1) // TS
    reps = TS // LANES
    nt_dims = (((1,), (1,)), ((), ()))

    t_idx = t0 + lax.broadcasted_iota(jnp.int32, (TQ, 1), 0)
    chunk_end = (t_idx // CHUNK + 1) * CHUNK - 1

    def admissible(kt):
        s_idx = kt * TS + lax.broadcasted_iota(jnp.int32, (1, TS), 1)
        return s_idx <= chunk_end

    def wexp_at(h):
        return (h // (PAIR * reps), pl.ds(((h // reps) % PAIR) * TQ, TQ), pl.ds((h % reps) * LANES, LANES))

    @pl.when(g == 0)
    def _select():
        w_all = wi_ref[0] * (IDX_HEAD_DIM ** -0.5)
        lane = lax.broadcasted_iota(jnp.int32, (TQ, LANES), 1)
        for h in range(NI):
            stash_ref[wexp_at(h)] = jnp.broadcast_to(w_all[:, h:h + 1], (TQ, LANES))
            slab = qi_ref[0, :, (h // 2) * LANES:(h // 2 + 1) * LANES]
            keep = (lane < IDX_HEAD_DIM) if h % 2 == 0 else (lane >= IDX_HEAD_DIM)
            slab_ref[h] = jnp.where(keep, slab, jnp.zeros_like(slab))

        def score_tile(kt, carry):
            s0 = pl.multiple_of(kt * TS, TS)
            kk = kk_ref[0, pl.ds(s0, TS), :]
            acc = jnp.zeros((TQ, TS), F32)
            for h in range(NI):
                lg = lax.dot_general(slab_ref[h], kk, nt_dims, preferred_element_type=F32)
                w = jnp.concatenate([stash_ref[wexp_at(h)]] * reps, axis=1)
                acc = acc + w * jnp.maximum(lg, 0.0)
            sc = jnp.where(admissible(kt), acc, -jnp.inf)
            bits = pltpu.bitcast(sc, jnp.int32)
            key_ref[kt] = bits ^ ((bits >> 31) & jnp.int32(0x7FFFFFFF))
            return carry

        lax.fori_loop(0, n_kt, score_tile, 0)

        def count_keys(pred):
            lane_idx = lax.broadcasted_iota(jnp.int32, (COUNT_ROWS, LANES), 1)
            counts = []
            for rb in range(TQ // COUNT_ROWS):
                rows = pl.ds(rb * COUNT_ROWS, COUNT_ROWS)

                def count_tile(kt, c, rows=rows, rb=rb):
                    keys = key_ref[kt, rows, :]
                    for r in range(reps):
                        s_idx = kt * TS + r * LANES + lane_idx
                        c = c + pred(keys[:, r * LANES:(r + 1) * LANES], s_idx, rb).astype(jnp.int32)
                    return c

                counts.append(_fori_unrolled(n_kt, count_tile, jnp.zeros((COUNT_ROWS, LANES), jnp.int32), 2))
            per_lane = jnp.concatenate(counts, axis=0).astype(F32)
            return jnp.sum(per_lane, axis=1, keepdims=True).astype(jnp.int32)

        def rows_of(x, rb):
            return x[rb * COUNT_ROWS:(rb + 1) * COUNT_ROWS]

        def bit_step(i, carry):
            thr, n_ge = carry
            cand = thr ^ (jnp.int32(1) << (31 - i))
            tot = count_keys(lambda keys, s_idx, rb: keys >= rows_of(cand, rb))
            ok = tot >= topk
            return jnp.where(ok, cand, thr), jnp.where(ok, tot, n_ge)

        thr, n_ge = lax.fori_loop(0, 32, bit_step, (jnp.full((TQ, LANES), INT32_MIN, jnp.int32),
                                                    jnp.full((TQ, 1), n_kt * TS, jnp.int32)))
        thr_w = jnp.concatenate([thr] * reps, axis=1)
        surplus_ties = jnp.max(n_ge) > topk

        def write_bias(select):
            def bias_tile(kt, carry):
                bias = jnp.where(select(key_ref[kt], kt) & admissible(kt), 0.0, -jnp.inf).astype(F32)
                key_ref[kt] = pltpu.bitcast(bias, jnp.int32)
                return carry

            lax.fori_loop(0, n_kt, bias_tile, 0)

        @pl.when(jnp.logical_not(surplus_ties))
        def _():
            write_bias(lambda keys, kt: keys >= thr_w)

        @pl.when(surplus_ties)
        def _():
            n_gt = count_keys(lambda keys, s_idx, rb: keys > rows_of(thr, rb))
            need = topk - n_gt

            def idx_step(i, last):
                cand = last | (jnp.int32(1) << (idx_bits - 1 - i))
                below = count_keys(
                    lambda keys, s_idx, rb: (keys == rows_of(thr, rb)) & (s_idx < rows_of(cand, rb)))
                return jnp.where(below < need, cand, last)

            last = lax.fori_loop(0, idx_bits, idx_step, jnp.zeros((TQ, LANES), jnp.int32))
            last_w = jnp.concatenate([last] * reps, axis=1)

            def select(keys, kt):
                s_idx = kt * TS + lax.broadcasted_iota(jnp.int32, (TQ, TS), 1)
                return (keys > thr_w) | ((keys == thr_w) & (s_idx <= last_w))

            write_bias(select)

    scale = HEAD_DIM ** -0.5
    ones_blk = jnp.ones((TS, HEAD_DIM), BF16)
    for pr in range(GROUP // PAIR):
        heads = range(pr * PAIR, (pr + 1) * PAIR)
        qs = jnp.concatenate([q_ref[0, :, c * HEAD_DIM:(c + 1) * HEAD_DIM] for c in heads], axis=0)
        mp_ref[...] = jnp.full_like(mp_ref, -jnp.inf)

        def logits_tile(kt):
            s0 = pl.multiple_of(kt * TS, TS)
            s = lax.dot_general(qs, k_ref[0, pl.ds(s0, TS), :], nt_dims, preferred_element_type=F32) * scale
            bias = pltpu.bitcast(key_ref[kt], F32)
            s = s + jnp.concatenate([bias] * PAIR, axis=0)
            stash_ref[kt] = s
            mp = mp_ref[...]
            for r in range(reps):
                mp = jnp.maximum(mp, s[:, r * LANES:(r + 1) * LANES])
            mp_ref[...] = mp

        _for_tiles(n_kt, logits_tile)
        m = jnp.max(mp_ref[...], axis=1, keepdims=True)
        m = jnp.where(m == -jnp.inf, 0.0, m)
        m_w = jnp.broadcast_to(m, (PAIR * TQ, TS))
        acc_ref[...] = jnp.zeros_like(acc_ref)

        def pv_tile(kt):
            s0 = pl.multiple_of(kt * TS, TS)
            p = jnp.exp(stash_ref[kt] - m_w).astype(BF16)
            v1 = jnp.concatenate([v_ref[0, pl.ds(s0, TS), :], ones_blk], axis=1)
            acc_ref[...] += jnp.dot(p, v1, preferred_element_type=F32)

        _for_tiles(n_kt, pv_tile)
        acc = acc_ref[...]
        o = acc[:, :HEAD_DIM] / acc[:, HEAD_DIM:]
        for i, c in enumerate(heads):
            o_ref[0, :, c * HEAD_DIM:(c + 1) * HEAD_DIM] = o[i * TQ:(i + 1) * TQ].astype(o_ref.dtype)


def _dsa_attention(qkv, qi, wi, kk, *, n_heads, topk, TQ=256, TS=512):
    B, S, _ = qkv.shape
    n_kv = n_heads // GROUP
    NI = wi.shape[-1]
    TQ = _pick(S, TQ)
    TS = _pick(S, TS)
    n_tiles = S // TS
    assert NI <= n_tiles * PAIR * (TS // LANES), "head-weight table does not fit in the logits stash"
    kern = functools.partial(_dsa_kernel, TQ=TQ, TS=TS, NI=NI, topk=topk, idx_bits=(S - 1).bit_length())
    gw = GROUP * HEAD_DIM
    return pl.pallas_call(
        kern,
        grid=(B, S // TQ, n_kv),
        in_specs=[
            pl.BlockSpec((1, TQ, gw), lambda b, i, g: (b, i, g)),
            pl.BlockSpec((1, S, HEAD_DIM), lambda b, i, g: (b, 0, n_heads + g)),
            pl.BlockSpec((1, S, HEAD_DIM), lambda b, i, g: (b, 0, n_heads + n_kv + g)),
            pl.BlockSpec((1, TQ, NI * IDX_HEAD_DIM), lambda b, i, g: (b, i, 0)),
            pl.BlockSpec((1, TQ, NI), lambda b, i, g: (b, i, 0)),
            pl.BlockSpec((1, S, LANES), lambda b, i, g: (b, 0, 0)),
        ],
        out_specs=pl.BlockSpec((1, TQ, gw), lambda b, i, g: (b, i, g)),
        out_shape=jax.ShapeDtypeStruct((B, S, n_heads * HEAD_DIM), BF16),
        scratch_shapes=[
            pltpu.VMEM((n_tiles, TQ, TS), jnp.int32),
            pltpu.VMEM((n_tiles, PAIR * TQ, TS), F32),
            pltpu.VMEM((NI, TQ, LANES), BF16),
            pltpu.VMEM((PAIR * TQ, LANES), F32),
            pltpu.VMEM((PAIR * TQ, 2 * HEAD_DIM), F32),
        ],
        compiler_params=_compiler_params(("parallel", "parallel", "arbitrary")),
        name="dsa_attention",
    )(qkv, qkv, qkv, qi, wi, kk)


SUB = 16


def _hgrn_kernel(q_ref, f_ref, v_ref, gate_ref, ng_ref, o_ref, state_ref, *, TT, HG):
    @pl.when(pl.program_id(2) == 0)
    def _():
        state_ref[...] = jnp.zeros_like(state_ref)

    C = CHUNK
    row = lax.broadcasted_iota(jnp.int32, (C, C), 0)
    col = lax.broadcasted_iota(jnp.int32, (C, C), 1)
    tril = (row >= col).astype(F32)
    n_sub = C // SUB
    sub_row = lax.broadcasted_iota(jnp.int32, (SUB, C), 0)
    sub_col = lax.broadcasted_iota(jnp.int32, (SUB, C), 1)
    nt_dims = (((1,), (1,)), ((), ()))
    ng = ng_ref[...]

    def chunk_body(c, carry):
        r0 = pl.multiple_of(c * C, C)
        for h in range(HG):
            cols = slice(h * HGRN_EXPAND, (h + 1) * HGRN_EXPAND)
            q = q_ref[0, pl.ds(r0, C), cols]
            f = f_ref[0, pl.ds(r0, C), cols]
            v = v_ref[0, pl.ds(r0, C), cols]
            gate = gate_ref[0, pl.ds(r0, C), cols]
            lf = jnp.log(f)
            kin = 1.0 - f
            b = jnp.dot(tril, lf, preferred_element_type=F32, precision=lax.Precision.HIGHEST)
            v16 = v.astype(BF16)
            st = state_ref[h]
            o = lax.dot_general((q * jnp.exp(b)).astype(BF16), st.astype(BF16), nt_dims,
                                preferred_element_type=F32)
            qparts, kparts = [], []
            for i in range(1, n_sub):
                lo, hi = i * SUB, (i + 1) * SUB
                ref = b[lo - 1:lo, :]
                q_i = q[lo:hi] * jnp.exp(b[lo:hi] - ref)
                k_i = kin[:lo] * jnp.exp(ref - b[:lo])
                tail = [jnp.zeros((C - hi, HGRN_EXPAND), F32)] if hi < C else []
                qparts.append(jnp.concatenate([jnp.zeros((lo, HGRN_EXPAND), F32), q_i] + tail, axis=0))
                kparts.append(jnp.concatenate([k_i, jnp.zeros((C - lo, HGRN_EXPAND), F32)], axis=0))
            qcat = jnp.concatenate(qparts, axis=1).astype(BF16)
            kcat = jnp.concatenate(kparts, axis=1).astype(BF16)
            a_off = lax.dot_general(qcat, kcat, nt_dims, preferred_element_type=F32)
            dparts = []
            for i in range(n_sub):
                blk = slice(i * SUB, (i + 1) * SUB)
                bt, qt, kt_ = b[blk], q[blk], kin[blk]
                a = jnp.zeros((SUB, C), F32)
                for s in range(SUB):
                    e = jnp.exp(bt - bt[s:s + 1, :])
                    a_col = jnp.sum(qt * kt_[s:s + 1, :] * e, axis=-1, keepdims=True)
                    a = jnp.where((sub_col == i * SUB + s) & (sub_row >= s), a_col, a)
                dparts.append(a)
            a_all = a_off + jnp.concatenate(dparts, axis=0)
            o = o + jnp.dot(a_all.astype(BF16), v16, preferred_element_type=F32)
            b_last = b[C - 1:C, :]
            kd = (kin * jnp.exp(b_last - b)).astype(BF16)
            upd = lax.dot_general(v16, kd, (((0,), (0,)), ((), ())), preferred_element_type=F32)
            state_ref[h] = jnp.exp(b_last) * st + upd
            ms = jnp.mean(o * o, axis=-1, keepdims=True)
            o_ref[0, pl.ds(r0, C), cols] = (o * lax.rsqrt(ms + RMS_EPS) * ng * gate).astype(o_ref.dtype)
        return carry

    def two_chunks(i, carry):
        chunk_body(2 * i, carry)
        chunk_body(2 * i + 1, carry)
        return carry

    lax.fori_loop(0, TT // (2 * C), two_chunks, 0)


def _hgrn_scan(q, f, v, gate, norm_g, *, TT=512, HG=8):
    B, S, D = q.shape
    H = D // HGRN_EXPAND
    TT = _pick(S, TT)
    HG = _pick(H, HG)
    blk = pl.BlockSpec((1, TT, HG * HGRN_EXPAND), lambda b, h, t: (b, t, h))
    return pl.pallas_call(
        functools.partial(_hgrn_kernel, TT=TT, HG=HG),
        grid=(B, H // HG, S // TT),
        in_specs=[blk, blk, blk, blk, pl.BlockSpec((1, HGRN_EXPAND), lambda b, h, t: (0, 0))],
        out_specs=blk,
        out_shape=jax.ShapeDtypeStruct((B, S, D), BF16),
        scratch_shapes=[pltpu.VMEM((HG, HGRN_EXPAND, HGRN_EXPAND), F32)],
        compiler_params=_compiler_params(("parallel", "parallel", "arbitrary")),
        name="hgrn_scan",
    )(q, f, v, gate, norm_g.reshape(1, HGRN_EXPAND))


def _rope_tables(seq, dim):
    inv = 1.0 / (ROPE_THETA ** (jnp.arange(0, dim, 2, dtype=F32) / dim))
    ang = jnp.arange(seq, dtype=F32)[:, None] * inv[None, :]
    cos, sin = jnp.cos(ang), jnp.sin(ang)
    reps = LANES // dim
    cos_full = jnp.tile(jnp.concatenate([cos, cos], axis=-1), (1, reps))
    sin_signed = jnp.tile(jnp.concatenate([-sin, sin], axis=-1), (1, reps))
    return cos_full, sin_signed


def _dsa_mixer(xb, B, S, w_in, w_out, kln_g, kln_b):
    D = xb.shape[1]
    n_heads = D // HEAD_DIM
    n_kv = n_heads // GROUP
    d_qkv = (n_heads + 2 * n_kv) * HEAD_DIM
    NI = (w_in[0].shape[-1] - d_qkv - IDX_HEAD_DIM) // (IDX_HEAD_DIM + 1)
    d_qi = NI * IDX_HEAD_DIM
    topk = min(TOPK_MAX, S // 4)
    cos128, sin128 = _rope_tables(S, HEAD_DIM)
    cos64, sin64 = _rope_tables(S, IDX_HEAD_DIM)

    qkv = _matmul(xb, w_in, cols=(0, d_qkv), kind="qkv", out_dtype=BF16, row_aux=(cos128, sin128),
                  static_aux=(n_heads + n_kv,), seq=S)
    qi = _matmul(xb, w_in, cols=(d_qkv, d_qkv + d_qi), kind="rope64", out_dtype=BF16,
                 row_aux=(cos64, sin64), seq=S)
    w_tail_src = w_in[0][w_in[1], :, d_qkv + d_qi:]
    w_wi = w_tail_src[:, :NI]
    w_ki = w_tail_src[:, NI:]
    pad = LANES - IDX_HEAD_DIM - NI
    w_tail = jnp.concatenate([w_ki, w_wi, jnp.zeros((D, pad), BF16)], axis=1)
    g_pad = jnp.pad(kln_g.astype(F32), (0, LANES - IDX_HEAD_DIM)).reshape(1, LANES)
    b_pad = jnp.pad(kln_b.astype(F32), (0, LANES - IDX_HEAD_DIM)).reshape(1, LANES)
    tail = _matmul(xb, w_tail, kind="idx_tail", out_dtype=F32, row_aux=(cos64, sin64),
                   col_aux=(g_pad, b_pad), static_aux=(NI ** -0.5,), seq=S)
    ki = tail[:, :IDX_HEAD_DIM].astype(BF16).reshape(B, S, IDX_HEAD_DIM)
    wi = tail[:, IDX_HEAD_DIM:IDX_HEAD_DIM + NI].reshape(B, S, NI)
    kk = jnp.concatenate([ki, ki], axis=-1)

    o = _dsa_attention(qkv.reshape(B, S, d_qkv), qi.reshape(B, S, d_qi), wi, kk, n_heads=n_heads, topk=topk)
    return _matmul(o.reshape(B * S, D), w_out)


def _hgrn_mixer(xb, B, S, w_in, lb, norm_g, w_out):
    D = xb.shape[1]
    q = _matmul(xb, w_in, cols=(0, D), kind="silu")
    f = _matmul(xb, w_in, cols=(D, 2 * D), kind="fgate", col_aux=(lb.reshape(1, D),))
    v = _matmul(xb, w_in, cols=(2 * D, 3 * D))
    gate = _matmul(xb, w_in, cols=(3 * D, 4 * D), kind="silu")
    shp = (B, S, D)
    o = _hgrn_scan(q.reshape(shp), f.reshape(shp), v.reshape(shp), gate.reshape(shp), norm_g)
    return _matmul(o.reshape(B * S, D), w_out)


def _mlp(xb, w_up, w_down):
    h = _matmul(xb, w_up, kind="relu2", out_dtype=BF16)
    return _matmul(h, w_down)


def kernel(x, attn_w_in, attn_w_out, idx_k_ln_g, idx_k_ln_b, hgrn_w_in, hgrn_lower_bounds, hgrn_norm_g,
           hgrn_w_out, mix_ln_g, mix_ln_b, mlp_w_up, mlp_w_down, mlp_ln_g, mlp_ln_b):
    B, S, D = x.shape
    depth = mix_ln_g.shape[0]
    alpha = (2.0 * depth) ** 0.25
    lb_all = jnp.cumsum(jax.nn.softmax(hgrn_lower_bounds.astype(F32), axis=0), axis=0)
    lb_all = lb_all - lb_all[0:1]
    attn_in16, attn_out16 = attn_w_in.astype(BF16), attn_w_out.astype(BF16)
    hgrn_in16, hgrn_out16 = hgrn_w_in.astype(BF16), hgrn_w_out.astype(BF16)
    up16, down16 = mlp_w_up.astype(BF16), mlp_w_down.astype(BF16)
    xf = x.reshape(B * S, D)
    xb = xf.astype(BF16)
    for layer in range(depth):
        j = layer // 2
        if layer % 2 == 0:
            h = _dsa_mixer(xb, B, S, (attn_in16, j), (attn_out16, j), idx_k_ln_g[j], idx_k_ln_b[j])
        else:
            h = _hgrn_mixer(xb, B, S, (hgrn_in16, j), lb_all[layer], hgrn_norm_g[j], (hgrn_out16, j))
        xf, xb = _residual_ln(xf, h, mix_ln_g[layer], mix_ln_b[layer], alpha)
        h = _mlp(xb, (up16, layer), (down16, layer))
        xf, xb = _residual_ln(xf, h, mlp_ln_g[layer], mlp_ln_b[layer], alpha)
    return xf.reshape(B, S, D)
```

```python
import functools
import math

import jax
import jax.numpy as jnp
from jax import lax
from jax.experimental import pallas as pl
from jax.experimental.pallas import tpu as pltpu

HEAD_DIM = 128
GROUP = 4
IDX_HEAD_DIM = 64
TOPK_MAX = 256
CHUNK = 64
ROPE_THETA = 10000.0
HGRN_EXPAND = 128
LN_EPS = 1e-5
RMS_EPS = 1e-6

V7X_VMEM_BYTES = 64 * 1024 * 1024
VMEM_LIMIT_BYTES = V7X_VMEM_BYTES * 7 // 8
LANES = 128

F32 = jnp.float32
BF16 = jnp.bfloat16
INT32_MIN = -(2 ** 31)


def _compiler_params(semantics):
    return pltpu.CompilerParams(dimension_semantics=semantics, vmem_limit_bytes=VMEM_LIMIT_BYTES)


def _pick(n, pref):
    t = min(n, pref)
    while n % t:
        t //= 2
    return t


def _rope128(x, cos, sin):
    return x * cos + pltpu.roll(x, 64, 1) * sin


def _rope64(x, cos, sin):
    lane = lax.broadcasted_iota(jnp.int32, x.shape, 1)
    first_half = (lane % 64) < 32
    partner = jnp.where(first_half, pltpu.roll(x, 96, 1), pltpu.roll(x, 32, 1))
    return x * cos + partner * sin


def _apply_epilogue(kind, acc, aux, j):
    tn = acc.shape[1]
    if kind == "none":
        return acc
    if kind == "relu2":
        r = jnp.maximum(acc, 0.0)
        return r * r
    if kind == "silu":
        return acc * jax.nn.sigmoid(acc)
    if kind == "fgate":
        lb = aux[0][...]
        return lb + (1.0 - lb) * jax.nn.sigmoid(acc)
    if kind in ("rope128", "rope64"):
        cos = aux[0][...]
        sin = aux[1][...]
        fn = _rope128 if kind == "rope128" else _rope64
        parts = [fn(acc[:, c * LANES:(c + 1) * LANES], cos, sin) for c in range(tn // LANES)]
        return jnp.concatenate(parts, axis=1)
    if kind == "qkv":
        cos = aux[0][...]
        sin = aux[1][...]
        n_rope = aux[2]
        parts = []
        for c in range(tn // LANES):
            x = acc[:, c * LANES:(c + 1) * LANES]
            parts.append(jnp.where(j * (tn // LANES) + c < n_rope, _rope128(x, cos, sin), x))
        return jnp.concatenate(parts, axis=1)
    if kind == "idx_tail":
        cos, sin, g, b, wscale = aux[0][...], aux[1][...], aux[2][...], aux[3][...], aux[4]
        lane = lax.broadcasted_iota(jnp.int32, acc.shape, 1)
        is_k = lane < IDX_HEAD_DIM
        mu = jnp.sum(jnp.where(is_k, acc, 0.0), axis=1, keepdims=True) * (1.0 / IDX_HEAD_DIM)
        d = jnp.where(is_k, acc - mu, 0.0)
        var = jnp.sum(d * d, axis=1, keepdims=True) * (1.0 / IDX_HEAD_DIM)
        y = d * lax.rsqrt(var + LN_EPS) * g + b
        y = _rope64(y, cos, sin)
        return jnp.where(is_k, y, acc * wscale)
    raise ValueError(kind)


def _mm_kernel(*refs, nk, kind, n_aux_refs, static_aux):
    a_ref, w_ref = refs[0], refs[1]
    aux_refs = list(refs[2:2 + n_aux_refs]) + list(static_aux)
    o_ref = refs[2 + n_aux_refs]
    j = pl.program_id(1)
    if nk == 1:
        acc = jnp.dot(a_ref[...], w_ref[...], preferred_element_type=F32)
        o_ref[...] = _apply_epilogue(kind, acc, aux_refs, j).astype(o_ref.dtype)
        return
    acc_ref = refs[3 + n_aux_refs]
    k = pl.program_id(2)

    @pl.when(k == 0)
    def _():
        acc_ref[...] = jnp.zeros_like(acc_ref)

    acc_ref[...] += jnp.dot(a_ref[...], w_ref[...], preferred_element_type=F32)

    @pl.when(k == nk - 1)
    def _():
        o_ref[...] = _apply_epilogue(kind, acc_ref[...], aux_refs, j).astype(o_ref.dtype)


def _matmul(a, w, *, kind="none", out_dtype=F32, row_aux=(), col_aux=(), static_aux=(),
            seq=None, cols=None, tm=1024, tn=1024, tk=4096):
    M, K = a.shape
    layer = None
    if isinstance(w, tuple):
        w, layer = w
    lo, hi = (0, w.shape[-1]) if cols is None else cols
    N = hi - lo
    tm = _pick(M if seq is None else seq, tm)
    tn = _pick(math.gcd(N, lo) if lo else N, tn)
    tk = _pick(K, tk)
    nk = K // tk
    j0 = lo // tn
    grid = (M // tm, N // tn, nk)
    if layer is None:
        w_spec = pl.BlockSpec((tk, tn), lambda i, j, k: (k, j + j0))
    else:
        w_spec = pl.BlockSpec((None, tk, tn), lambda i, j, k: (layer, k, j + j0))
    in_specs = [pl.BlockSpec((tm, tk), lambda i, j, k: (i, k)), w_spec]
    args = [a, w]
    for t in row_aux:
        nrow = seq // tm
        in_specs.append(pl.BlockSpec((tm, t.shape[1]), lambda i, j, k, nrow=nrow: (i % nrow, 0)))
        args.append(t)
    for t in col_aux:
        in_specs.append(pl.BlockSpec((1, tn), lambda i, j, k: (0, j)))
        args.append(t)
    scratch = [] if nk == 1 else [pltpu.VMEM((tm, tn), F32)]
    kern = functools.partial(_mm_kernel, nk=nk, kind=kind, n_aux_refs=len(row_aux) + len(col_aux),
                             static_aux=tuple(static_aux))
    return pl.pallas_call(
        kern,
        grid=grid,
        in_specs=in_specs,
        out_specs=pl.BlockSpec((tm, tn), lambda i, j, k: (i, j)),
        out_shape=jax.ShapeDtypeStruct((M, N), out_dtype),
        scratch_shapes=scratch,
        compiler_params=_compiler_params(("parallel", "parallel", "arbitrary")),
        name="mm_" + kind,
    )(*args)


def _ln_kernel(x_ref, h_ref, g_ref, b_ref, o_ref, ob_ref, *, alpha):
    y = alpha * x_ref[...] + h_ref[...]
    mu = jnp.mean(y, axis=-1, keepdims=True)
    d = y - mu
    var = jnp.mean(d * d, axis=-1, keepdims=True)
    out = d * lax.rsqrt(var + LN_EPS) * g_ref[...] + b_ref[...]
    o_ref[...] = out
    ob_ref[...] = out.astype(BF16)


def _residual_ln(x, h, g, b, alpha, tm=256):
    M, D = x.shape
    tm = _pick(M, tm)
    row = pl.BlockSpec((tm, D), lambda i: (i, 0))
    vec = pl.BlockSpec((1, D), lambda i: (0, 0))
    return pl.pallas_call(
        functools.partial(_ln_kernel, alpha=alpha),
        grid=(M // tm,),
        in_specs=[row, row, vec, vec],
        out_specs=[row, row],
        out_shape=[jax.ShapeDtypeStruct((M, D), F32), jax.ShapeDtypeStruct((M, D), BF16)],
        compiler_params=_compiler_params(("parallel",)),
        name="residual_ln",
    )(x, h, g.reshape(1, D), b.reshape(1, D))


PAIR = 2


COUNT_ROWS = 128


def _fori_unrolled(n, body, init, unroll):
    def many(i, carry):
        for u in range(unroll):
            carry = body(unroll * i + u, carry)
        return carry

    carry = lax.fori_loop(0, n // unroll, many, init)
    return lax.fori_loop((n // unroll) * unroll, n, body, carry)


def _for_tile_groups(n_groups, body, group):
    def many(i, carry):
        for u in range(group):
            body(group * i + u)
        return carry

    lax.fori_loop(0, n_groups, many, 0)


def _dsa_kernel(q_ref, k_ref, v_ref, qi_ref, wi_ref, kk_ref, o_ref,
                key_ref, stash_ref, slab_ref, mp_ref, acc_ref, *, TQ, TS, NI, topk, idx_bits, ATT_GROUP):
    qb = pl.program_id(1)
    g = pl.program_id(2)
    t0 = qb * TQ
    n_kt = (t0 + TQ + TS - 1) // TS
    reps = TS // LANES
    nt_dims = (((1,), (1,)), ((), ()))
    n_groups = (n_kt + ATT_GROUP - 1) // ATT_GROUP

    t_idx = t0 + lax.broadcasted_iota(jnp.int32, (TQ, 1), 0)
    chunk_end = (t_idx // CHUNK + 1) * CHUNK - 1

    def admissible(kt):
        s_idx = kt * TS + lax.broadcasted_iota(jnp.int32, (1, TS), 1)
        return s_idx <= chunk_end

    def wexp_at(h):
        return (h // (PAIR * reps), pl.ds(((h // reps) % PAIR) * TQ, TQ), pl.ds((h % reps) * LANES, LANES))

    @pl.when(g == 0)
    def _select():
        w_all = wi_ref[0] * (IDX_HEAD_DIM ** -0.5)
        lane = lax.broadcasted_iota(jnp.int32, (TQ, LANES), 1)
        for h in range(NI):
            stash_ref[wexp_at(h)] = jnp.broadcast_to(w_all[:, h:h + 1], (TQ, LANES))
            slab = qi_ref[0, :, (h // 2) * LANES:(h // 2 + 1) * LANES]
            keep = (lane < IDX_HEAD_DIM) if h % 2 == 0 else (lane >= IDX_HEAD_DIM)
            slab_ref[h] = jnp.where(keep, slab, jnp.zeros_like(slab))

        def score_tile(kt, carry):
            s0 = pl.multiple_of(kt * TS, TS)
            kk = kk_ref[0, pl.ds(s0, TS), :]
            acc = jnp.zeros((TQ, TS), F32)
            for h in range(NI):
                lg = lax.dot_general(slab_ref[h], kk, nt_dims, preferred_element_type=F32)
                w = jnp.concatenate([stash_ref[wexp_at(h)]] * reps, axis=1)
                acc = acc + w * jnp.maximum(lg, 0.0)
            sc = jnp.where(admissible(kt), acc, -jnp.inf)
            bits = pltpu.bitcast(sc, jnp.int32)
            key_ref[kt] = bits ^ ((bits >> 31) & jnp.int32(0x7FFFFFFF))
            return carry

        lax.fori_loop(0, n_kt, score_tile, 0)

        def count_keys(pred):
            lane_idx = lax.broadcasted_iota(jnp.int32, (COUNT_ROWS, LANES), 1)
            counts = []
            for rb in range(TQ // COUNT_ROWS):
                rows = pl.ds(rb * COUNT_ROWS, COUNT_ROWS)

                def count_tile(kt, c, rows=rows, rb=rb):
                    keys = key_ref[kt, rows, :]
                    for r in range(reps):
                        s_idx = kt * TS + r * LANES + lane_idx
                        c = c + pred(keys[:, r * LANES:(r + 1) * LANES], s_idx, rb).astype(jnp.int32)
                    return c

                counts.append(_fori_unrolled(n_kt, count_tile, jnp.zeros((COUNT_ROWS, LANES), jnp.int32), 2))
            per_lane = jnp.concatenate(counts, axis=0).astype(F32)
            return jnp.sum(per_lane, axis=1, keepdims=True).astype(jnp.int32)

        def rows_of(x, rb):
            return x[rb * COUNT_ROWS:(rb + 1) * COUNT_ROWS]

        def bit_step(i, carry):
            thr, n_ge = carry
            cand = thr ^ (jnp.int32(1) << (31 - i))
            tot = count_keys(lambda keys, s_idx, rb: keys >= rows_of(cand, rb))
            ok = tot >= topk
            return jnp.where(ok, cand, thr), jnp.where(ok, tot, n_ge)

        thr, n_ge = lax.fori_loop(0, 32, bit_step, (jnp.full((TQ, LANES), INT32_MIN, jnp.int32),
                                                    jnp.full((TQ, 1), n_kt * TS, jnp.int32)))
        thr_w = jnp.concatenate([thr] * reps, axis=1)
        surplus_ties = jnp.max(n_ge) > topk

        def write_bias(select):
            def bias_tile(kt, carry):
                bias = jnp.where(select(key_ref[kt], kt) & admissible(kt), 0.0, -jnp.inf).astype(F32)
                key_ref[kt] = pltpu.bitcast(bias, jnp.int32)
                return carry

            lax.fori_loop(0, n_kt, bias_tile, 0)

        def pad_tile(kt, carry):
            key_ref[kt] = pltpu.bitcast(jnp.full((TQ, TS), -jnp.inf, F32), jnp.int32)
            return carry

        lax.fori_loop(n_kt, n_groups * ATT_GROUP, pad_tile, 0)

        @pl.when(jnp.logical_not(surplus_ties))
        def _():
            write_bias(lambda keys, kt: keys >= thr_w)

        @pl.when(surplus_ties)
        def _():
            n_gt = count_keys(lambda keys, s_idx, rb: keys > rows_of(thr, rb))
            need = topk - n_gt

            def idx_step(i, last):
                cand = last | (jnp.int32(1) << (idx_bits - 1 - i))
                below = count_keys(
                    lambda keys, s_idx, rb: (keys == rows_of(thr, rb)) & (s_idx < rows_of(cand, rb)))
                return jnp.where(below < need, cand, last)

            last = lax.fori_loop(0, idx_bits, idx_step, jnp.zeros((TQ, LANES), jnp.int32))
            last_w = jnp.concatenate([last] * reps, axis=1)

            def select(keys, kt):
                s_idx = kt * TS + lax.broadcasted_iota(jnp.int32, (TQ, TS), 1)
                return (keys > thr_w) | ((keys == thr_w) & (s_idx <= last_w))

            write_bias(select)

    scale = HEAD_DIM ** -0.5
    ones_blk = jnp.ones((TS, HEAD_DIM), BF16)
    for pr in range(GROUP // PAIR):
        heads = range(pr * PAIR, (pr + 1) * PAIR)
        qs = jnp.concatenate([q_ref[0, :, c * HEAD_DIM:(c + 1) * HEAD_DIM] for c in heads], axis=0)
        mp_ref[...] = jnp.full_like(mp_ref, -jnp.inf)

        def logits_tile(kt):
            s0 = pl.multiple_of(kt * TS, TS)
            s = lax.dot_general(qs, k_ref[0, pl.ds(s0, TS), :], nt_dims, preferred_element_type=F32) * scale
            bias = pltpu.bitcast(key_ref[kt], F32)
            s = s + jnp.concatenate([bias] * PAIR, axis=0)
            stash_ref[kt] = s
            mp = mp_ref[...]
            for r in range(reps):
                mp = jnp.maximum(mp, s[:, r * LANES:(r + 1) * LANES])
            mp_ref[...] = mp

        _for_tile_groups(n_groups, logits_tile, ATT_GROUP)
        m = jnp.max(mp_ref[...], axis=1, keepdims=True)
        m = jnp.where(m == -jnp.inf, 0.0, m)
        m_w = jnp.broadcast_to(m, (PAIR * TQ, TS))
        acc_ref[...] = jnp.zeros_like(acc_ref)

        def pv_tile(kt):
            s0 = pl.multiple_of(kt * TS, TS)
            p = jnp.exp(stash_ref[kt] - m_w).astype(BF16)
            v1 = jnp.concatenate([v_ref[0, pl.ds(s0, TS), :], ones_blk], axis=1)
            acc_ref[...] += jnp.dot(p, v1, preferred_element_type=F32)

        _for_tile_groups(n_groups, pv_tile, ATT_GROUP)
        acc = acc_ref[...]
        o = acc[:, :HEAD_DIM] / acc[:, HEAD_DIM:]
        for i, c in enumerate(heads):
            o_ref[0, :, c * HEAD_DIM:(c + 1) * HEAD_DIM] = o[i * TQ:(i + 1) * TQ].astype(o_ref.dtype)


def _dsa_attention(qkv, qi, wi, kk, *, n_heads, topk, TQ=256, TS=512):
    B, S, _ = qkv.shape
    n_kv = n_heads // GROUP
    NI = wi.shape[-1]
    TQ = _pick(S, TQ)
    TS = _pick(S, TS)
    n_tiles = S // TS
    assert NI <= n_tiles * PAIR * (TS // LANES), "head-weight table does not fit in the logits stash"
    att_group = math.gcd(n_tiles, 4)
    kern = functools.partial(_dsa_kernel, TQ=TQ, TS=TS, NI=NI, topk=topk, idx_bits=(S - 1).bit_length(),
                             ATT_GROUP=att_group)
    gw = GROUP * HEAD_DIM
    return pl.pallas_call(
        kern,
        grid=(B, S // TQ, n_kv),
        in_specs=[
            pl.BlockSpec((1, TQ, gw), lambda b, i, g: (b, i, g)),
            pl.BlockSpec((1, S, HEAD_DIM), lambda b, i, g: (b, 0, n_heads + g)),
            pl.BlockSpec((1, S, HEAD_DIM), lambda b, i, g: (b, 0, n_heads + n_kv + g)),
            pl.BlockSpec((1, TQ, NI * IDX_HEAD_DIM), lambda b, i, g: (b, i, 0)),
            pl.BlockSpec((1, TQ, NI), lambda b, i, g: (b, i, 0)),
            pl.BlockSpec((1, S, LANES), lambda b, i, g: (b, 0, 0)),
        ],
        out_specs=pl.BlockSpec((1, TQ, gw), lambda b, i, g: (b, i, g)),
        out_shape=jax.ShapeDtypeStruct((B, S, n_heads * HEAD_DIM), BF16),
        scratch_shapes=[
            pltpu.VMEM((n_tiles, TQ, TS), jnp.int32),
            pltpu.VMEM((n_tiles, PAIR * TQ, TS), F32),
            pltpu.VMEM((NI, TQ, LANES), BF16),
            pltpu.VMEM((PAIR * TQ, LANES), F32),
            pltpu.VMEM((PAIR * TQ, 2 * HEAD_DIM), F32),
        ],
        compiler_params=_compiler_params(("parallel", "parallel", "arbitrary")),
        name="dsa_attention",
    )(qkv, qkv, qkv, qi, wi, kk)


SUB = 16


def _hgrn_kernel(q_ref, f_ref, v_ref, gate_ref, ng_ref, o_ref, state_ref, *, TT, HG):
    @pl.when(pl.program_id(2) == 0)
    def _():
        state_ref[...] = jnp.zeros_like(state_ref)

    C = CHUNK
    row = lax.broadcasted_iota(jnp.int32, (C, C), 0)
    col = lax.broadcasted_iota(jnp.int32, (C, C), 1)
    tril = (row >= col).astype(F32)
    n_sub = C // SUB
    sub_row = lax.broadcasted_iota(jnp.int32, (SUB, C), 0)
    sub_col = lax.broadcasted_iota(jnp.int32, (SUB, C), 1)
    nt_dims = (((1,), (1,)), ((), ()))
    ng = ng_ref[...]

    def chunk_body(c, carry):
        r0 = pl.multiple_of(c * C, C)
        for h in range(HG):
            cols = slice(h * HGRN_EXPAND, (h + 1) * HGRN_EXPAND)
            q = q_ref[0, pl.ds(r0, C), cols]
            f = f_ref[0, pl.ds(r0, C), cols]
            v = v_ref[0, pl.ds(r0, C), cols]
            gate = gate_ref[0, pl.ds(r0, C), cols]
            lf = jnp.log(f)
            kin = 1.0 - f
            b = jnp.dot(tril, lf, preferred_element_type=F32, precision=lax.Precision.HIGHEST)
            v16 = v.astype(BF16)
            st = state_ref[h]
            o = lax.dot_general((q * jnp.exp(b)).astype(BF16), st.astype(BF16), nt_dims,
                                preferred_element_type=F32)
            qparts, kparts = [], []
            for i in range(1, n_sub):
                lo, hi = i * SUB, (i + 1) * SUB
                ref = b[lo - 1:lo, :]
                q_i = q[lo:hi] * jnp.exp(b[lo:hi] - ref)
                k_i = kin[:lo] * jnp.exp(ref - b[:lo])
                tail = [jnp.zeros((C - hi, HGRN_EXPAND), F32)] if hi < C else []
                qparts.append(jnp.concatenate([jnp.zeros((lo, HGRN_EXPAND), F32), q_i] + tail, axis=0))
                kparts.append(jnp.concatenate([k_i, jnp.zeros((C - lo, HGRN_EXPAND), F32)], axis=0))
            qcat = jnp.concatenate(qparts, axis=1).astype(BF16)
            kcat = jnp.concatenate(kparts, axis=1).astype(BF16)
            a_off = lax.dot_general(qcat, kcat, nt_dims, preferred_element_type=F32)
            dparts = []
            for i in range(n_sub):
                blk = slice(i * SUB, (i + 1) * SUB)
                bt, qt, kt_ = b[blk], q[blk], kin[blk]
                a = jnp.zeros((SUB, C), F32)
                for s in range(SUB):
                    e = jnp.exp(bt - bt[s:s + 1, :])
                    a_col = jnp.sum(qt * kt_[s:s + 1, :] * e, axis=-1, keepdims=True)
                    a = jnp.where((sub_col == i * SUB + s) & (sub_row >= s), a_col, a)
                dparts.append(a)
            a_all = a_off + jnp.concatenate(dparts, axis=0)
            o = o + jnp.dot(a_all.astype(BF16), v16, preferred_element_type=F32)
            b_last = b[C - 1:C, :]
            kd = (kin * jnp.exp(b_last - b)).astype(BF16)
            upd = lax.dot_general(v16, kd, (((0,), (0,)), ((), ())), preferred_element_type=F32)
            state_ref[h] = jnp.exp(b_last) * st + upd
            ms = jnp.mean(o * o, axis=-1, keepdims=True)
            o_ref[0, pl.ds(r0, C), cols] = (o * lax.rsqrt(ms + RMS_EPS) * ng * gate).astype(o_ref.dtype)
        return carry

    def two_chunks(i, carry):
        chunk_body(2 * i, carry)
        chunk_body(2 * i + 1, carry)
        return carry

    lax.fori_loop(0, TT // (2 * C), two_chunks, 0)


def _hgrn_scan(q, f, v, gate, norm_g, *, TT=512, HG=8):
    B, S, D = q.shape
    H = D // HGRN_EXPAND
    TT = _pick(S, TT)
    HG = _pick(H, HG)
    blk = pl.BlockSpec((1, TT, HG * HGRN_EXPAND), lambda b, h, t: (b, t, h))
    return pl.pallas_call(
        functools.partial(_hgrn_kernel, TT=TT, HG=HG),
        grid=(B, H // HG, S // TT),
        in_specs=[blk, blk, blk, blk, pl.BlockSpec((1, HGRN_EXPAND), lambda b, h, t: (0, 0))],
        out_specs=blk,
        out_shape=jax.ShapeDtypeStruct((B, S, D), BF16),
        scratch_shapes=[pltpu.VMEM((HG, HGRN_EXPAND, HGRN_EXPAND), F32)],
        compiler_params=_compiler_params(("parallel", "parallel", "arbitrary")),
        name="hgrn_scan",
    )(q, f, v, gate, norm_g.reshape(1, HGRN_EXPAND))


def _rope_tables(seq, dim):
    inv = 1.0 / (ROPE_THETA ** (jnp.arange(0, dim, 2, dtype=F32) / dim))
    ang = jnp.arange(seq, dtype=F32)[:, None] * inv[None, :]
    cos, sin = jnp.cos(ang), jnp.sin(ang)
    reps = LANES // dim
    cos_full = jnp.tile(jnp.concatenate([cos, cos], axis=-1), (1, reps))
    sin_signed = jnp.tile(jnp.concatenate([-sin, sin], axis=-1), (1, reps))
    return cos_full, sin_signed


def _dsa_mixer(xb, B, S, w_in, w_out, kln_g, kln_b):
    D = xb.shape[1]
    n_heads = D // HEAD_DIM
    n_kv = n_heads // GROUP
    d_qkv = (n_heads + 2 * n_kv) * HEAD_DIM
    NI = (w_in[0].shape[-1] - d_qkv - IDX_HEAD_DIM) // (IDX_HEAD_DIM + 1)
    d_qi = NI * IDX_HEAD_DIM
    topk = min(TOPK_MAX, S // 4)
    cos128, sin128 = _rope_tables(S, HEAD_DIM)
    cos64, sin64 = _rope_tables(S, IDX_HEAD_DIM)

    qkv = _matmul(xb, w_in, cols=(0, d_qkv), kind="qkv", out_dtype=BF16, row_aux=(cos128, sin128),
                  static_aux=(n_heads + n_kv,), seq=S)
    qi = _matmul(xb, w_in, cols=(d_qkv, d_qkv + d_qi), kind="rope64", out_dtype=BF16,
                 row_aux=(cos64, sin64), seq=S)
    w_tail_src = w_in[0][w_in[1], :, d_qkv + d_qi:]
    w_wi = w_tail_src[:, :NI]
    w_ki = w_tail_src[:, NI:]
    pad = LANES - IDX_HEAD_DIM - NI
    w_tail = jnp.concatenate([w_ki, w_wi, jnp.zeros((D, pad), BF16)], axis=1)
    g_pad = jnp.pad(kln_g.astype(F32), (0, LANES - IDX_HEAD_DIM)).reshape(1, LANES)
    b_pad = jnp.pad(kln_b.astype(F32), (0, LANES - IDX_HEAD_DIM)).reshape(1, LANES)
    tail = _matmul(xb, w_tail, kind="idx_tail", out_dtype=F32, row_aux=(cos64, sin64),
                   col_aux=(g_pad, b_pad), static_aux=(NI ** -0.5,), seq=S)
    ki = tail[:, :IDX_HEAD_DIM].astype(BF16).reshape(B, S, IDX_HEAD_DIM)
    wi = tail[:, IDX_HEAD_DIM:IDX_HEAD_DIM + NI].reshape(B, S, NI)
    kk = jnp.concatenate([ki, ki], axis=-1)

    o = _dsa_attention(qkv.reshape(B, S, d_qkv), qi.reshape(B, S, d_qi), wi, kk, n_heads=n_heads, topk=topk)
    return _matmul(o.reshape(B * S, D), w_out)


def _hgrn_mixer(xb, B, S, w_in, lb, norm_g, w_out):
    D = xb.shape[1]
    q = _matmul(xb, w_in, cols=(0, D), kind="silu")
    f = _matmul(xb, w_in, cols=(D, 2 * D), kind="fgate", col_aux=(lb.reshape(1, D),))
    v = _matmul(xb, w_in, cols=(2 * D, 3 * D))
    gate = _matmul(xb, w_in, cols=(3 * D, 4 * D), kind="silu")
    shp = (B, S, D)
    o = _hgrn_scan(q.reshape(shp), f.reshape(shp), v.reshape(shp), gate.reshape(shp), norm_g)
    return _matmul(o.reshape(B * S, D), w_out)


def _mlp(xb, w_up, w_down):
    h = _matmul(xb, w_up, kind="relu2", out_dtype=BF16)
    return _matmul(h, w_down)


def kernel(x, attn_w_in, attn_w_out, idx_k_ln_g, idx_k_ln_b, hgrn_w_in, hgrn_lower_bounds, hgrn_norm_g,
           hgrn_w_out, mix_ln_g, mix_ln_b, mlp_w_up, mlp_w_down, mlp_ln_g, mlp_ln_b):
    B, S, D = x.shape
    depth = mix_ln_g.shape[0]
    alpha = (2.0 * depth) ** 0.25
    lb_all = jnp.cumsum(jax.nn.softmax(hgrn_lower_bounds.astype(F32), axis=0), axis=0)
    lb_all = lb_all - lb_all[0:1]
    attn_in16, attn_out16 = attn_w_in.astype(BF16), attn_w_out.astype(BF16)
    hgrn_in16, hgrn_out16 = hgrn_w_in.astype(BF16), hgrn_w_out.astype(BF16)
    up16, down16 = mlp_w_up.astype(BF16), mlp_w_down.astype(BF16)
    xf = x.reshape(B * S, D)
    xb = xf.astype(BF16)
    for layer in range(depth):
        j = layer // 2
        if layer % 2 == 0:
            h = _dsa_mixer(xb, B, S, (attn_in16, j), (attn_out16, j), idx_k_ln_g[j], idx_k_ln_b[j])
        else:
            h = _hgrn_mixer(xb, B, S, (hgrn_in16, j), lb_all[layer], hgrn_norm_g[j], (hgrn_out16, j))
        xf, xb = _residual_ln(xf, h, mix_ln_g[layer], mix_ln_b[layer], alpha)
        h = _mlp(xb, (up16, layer), (down16, layer))
        xf, xb = _residual_ln(xf, h, mlp_ln_g[layer], mlp_ln_b[layer], alpha)
    return xf.reshape(B, S, D)
```

```python
import functools
import math

import jax
import jax.numpy as jnp
from jax import lax
from jax.experimental import pallas as pl
from jax.experimental.pallas import tpu as pltpu

HEAD_DIM = 128
GROUP = 4
IDX_HEAD_DIM = 64
TOPK_MAX = 256
CHUNK = 64
ROPE_THETA = 10000.0
HGRN_EXPAND = 128
LN_EPS = 1e-5
RMS_EPS = 1e-6

V7X_VMEM_BYTES = 64 * 1024 * 1024
VMEM_LIMIT_BYTES = V7X_VMEM_BYTES * 7 // 8
LANES = 128

F32 = jnp.float32
BF16 = jnp.bfloat16
INT32_MIN = -(2 ** 31)


def _compiler_params(semantics):
    return pltpu.CompilerParams(dimension_semantics=semantics, vmem_limit_bytes=VMEM_LIMIT_BYTES)


def _pick(n, pref):
    t = min(n, pref)
    while n % t:
        t //= 2
    return t


def _rope128(x, cos, sin):
    return x * cos + pltpu.roll(x, 64, 1) * sin


def _rope64(x, cos, sin):
    lane = lax.broadcasted_iota(jnp.int32, x.shape, 1)
    first_half = (lane % 64) < 32
    partner = jnp.where(first_half, pltpu.roll(x, 96, 1), pltpu.roll(x, 32, 1))
    return x * cos + partner * sin


def _apply_epilogue(kind, acc, aux, j):
    tn = acc.shape[1]
    if kind == "none":
        return acc
    if kind == "relu2":
        r = jnp.maximum(acc, 0.0)
        return r * r
    if kind == "silu":
        return acc * jax.nn.sigmoid(acc)
    if kind == "fgate":
        lb = aux[0][...]
        return lb + (1.0 - lb) * jax.nn.sigmoid(acc)
    if kind in ("rope128", "rope64"):
        cos = aux[0][...]
        sin = aux[1][...]
        fn = _rope128 if kind == "rope128" else _rope64
        parts = [fn(acc[:, c * LANES:(c + 1) * LANES], cos, sin) for c in range(tn // LANES)]
        return jnp.concatenate(parts, axis=1)
    if kind == "qkv":
        cos = aux[0][...]
        sin = aux[1][...]
        n_rope = aux[2]
        parts = []
        for c in range(tn // LANES):
            x = acc[:, c * LANES:(c + 1) * LANES]
            parts.append(jnp.where(j * (tn // LANES) + c < n_rope, _rope128(x, cos, sin), x))
        return jnp.concatenate(parts, axis=1)
    if kind == "idx_tail":
        cos, sin, g, b, wscale = aux[0][...], aux[1][...], aux[2][...], aux[3][...], aux[4]
        lane = lax.broadcasted_iota(jnp.int32, acc.shape, 1)
        is_k = lane < IDX_HEAD_DIM
        mu = jnp.sum(jnp.where(is_k, acc, 0.0), axis=1, keepdims=True) * (1.0 / IDX_HEAD_DIM)
        d = jnp.where(is_k, acc - mu, 0.0)
        var = jnp.sum(d * d, axis=1, keepdims=True) * (1.0 / IDX_HEAD_DIM)
        y = d * lax.rsqrt(var + LN_EPS) * g + b
        y = _rope64(y, cos, sin)
        return jnp.where(is_k, y, acc * wscale)
    raise ValueError(kind)


def _mm_kernel(*refs, nk, kind, n_aux_refs, static_aux):
    a_ref, w_ref = refs[0], refs[1]
    aux_refs = list(refs[2:2 + n_aux_refs]) + list(static_aux)
    o_ref = refs[2 + n_aux_refs]
    j = pl.program_id(1)
    if nk == 1:
        acc = jnp.dot(a_ref[...], w_ref[...], preferred_element_type=F32)
        o_ref[...] = _apply_epilogue(kind, acc, aux_refs, j).astype(o_ref.dtype)
        return
    acc_ref = refs[3 + n_aux_refs]
    k = pl.program_id(2)

    @pl.when(k == 0)
    def _():
        acc_ref[...] = jnp.zeros_like(acc_ref)

    acc_ref[...] += jnp.dot(a_ref[...], w_ref[...], preferred_element_type=F32)

    @pl.when(k == nk - 1)
    def _():
        o_ref[...] = _apply_epilogue(kind, acc_ref[...], aux_refs, j).astype(o_ref.dtype)


def _matmul(a, w, *, kind="none", out_dtype=F32, row_aux=(), col_aux=(), static_aux=(),
            seq=None, cols=None, tm=1024, tn=1024, tk=4096):
    M, K = a.shape
    layer = None
    if isinstance(w, tuple):
        w, layer = w
    lo, hi = (0, w.shape[-1]) if cols is None else cols
    N = hi - lo
    tm = _pick(M if seq is None else seq, tm)
    tn = _pick(math.gcd(N, lo) if lo else N, tn)
    tk = _pick(K, tk)
    nk = K // tk
    j0 = lo // tn
    grid = (M // tm, N // tn, nk)
    if layer is None:
        w_spec = pl.BlockSpec((tk, tn), lambda i, j, k: (k, j + j0))
    else:
        w_spec = pl.BlockSpec((None, tk, tn), lambda i, j, k: (layer, k, j + j0))
    in_specs = [pl.BlockSpec((tm, tk), lambda i, j, k: (i, k)), w_spec]
    args = [a, w]
    for t in row_aux:
        nrow = seq // tm
        in_specs.append(pl.BlockSpec((tm, t.shape[1]), lambda i, j, k, nrow=nrow: (i % nrow, 0)))
        args.append(t)
    for t in col_aux:
        in_specs.append(pl.BlockSpec((1, tn), lambda i, j, k: (0, j)))
        args.append(t)
    scratch = [] if nk == 1 else [pltpu.VMEM((tm, tn), F32)]
    kern = functools.partial(_mm_kernel, nk=nk, kind=kind, n_aux_refs=len(row_aux) + len(col_aux),
                             static_aux=tuple(static_aux))
    return pl.pallas_call(
        kern,
        grid=grid,
        in_specs=in_specs,
        out_specs=pl.BlockSpec((tm, tn), lambda i, j, k: (i, j)),
        out_shape=jax.ShapeDtypeStruct((M, N), out_dtype),
        scratch_shapes=scratch,
        compiler_params=_compiler_params(("parallel", "parallel", "arbitrary")),
        name="mm_" + kind,
    )(*args)


def _ln_kernel(x_ref, h_ref, g_ref, b_ref, o_ref, ob_ref, *, alpha):
    y = alpha * x_ref[...] + h_ref[...]
    mu = jnp.mean(y, axis=-1, keepdims=True)
    d = y - mu
    var = jnp.mean(d * d, axis=-1, keepdims=True)
    out = d * lax.rsqrt(var + LN_EPS) * g_ref[...] + b_ref[...]
    o_ref[...] = out
    ob_ref[...] = out.astype(BF16)


def _residual_ln(x, h, g, b, alpha, tm=256):
    M, D = x.shape
    tm = _pick(M, tm)
    row = pl.BlockSpec((tm, D), lambda i: (i, 0))
    vec = pl.BlockSpec((1, D), lambda i: (0, 0))
    return pl.pallas_call(
        functools.partial(_ln_kernel, alpha=alpha),
        grid=(M // tm,),
        in_specs=[row, row, vec, vec],
        out_specs=[row, row],
        out_shape=[jax.ShapeDtypeStruct((M, D), F32), jax.ShapeDtypeStruct((M, D), BF16)],
        compiler_params=_compiler_params(("parallel",)),
        name="residual_ln",
    )(x, h, g.reshape(1, D), b.reshape(1, D))


PAIR = 2


COUNT_ROWS = 128


def _fori_unrolled(n, body, init, unroll):
    def many(i, carry):
        for u in range(unroll):
            carry = body(unroll * i + u, carry)
        return carry

    carry = lax.fori_loop(0, n // unroll, many, init)
    return lax.fori_loop((n // unroll) * unroll, n, body, carry)


def _for_tiles(n, body, unroll=8):
    def many(i, carry):
        for u in range(unroll):
            body(unroll * i + u)
        return carry

    lax.fori_loop(0, n // unroll, many, 0)
    width = unroll // 2
    while width >= 1:
        start = (n // (2 * width)) * (2 * width)

        @pl.when(n % (2 * width) >= width)
        def _(start=start, width=width):
            for u in range(width):
                body(start + u)

        width //= 2


def _dsa_kernel(q_ref, k_ref, v_ref, qi_ref, wi_ref, kk_ref, o_ref,
                key_ref, stash_ref, slab_ref, mp_ref, acc_ref, *, TQ, TS, NI, topk, idx_bits):
    qb = pl.program_id(1)
    g = pl.program_id(2)
    t0 = qb * TQ
    n_kt = (t0 + TQ + TS - 1) // TS
    reps = TS // LANES
    nt_dims = (((1,), (1,)), ((), ()))

    t_idx = t0 + lax.broadcasted_iota(jnp.int32, (TQ, 1), 0)
    chunk_end = (t_idx // CHUNK + 1) * CHUNK - 1

    def admissible(kt):
        s_idx = kt * TS + lax.broadcasted_iota(jnp.int32, (1, TS), 1)
        return s_idx <= chunk_end

    def wexp_at(h):
        return (h // (PAIR * reps), pl.ds(((h // reps) % PAIR) * TQ, TQ), pl.ds((h % reps) * LANES, LANES))

    @pl.when(g == 0)
    def _select():
        w_all = wi_ref[0] * (IDX_HEAD_DIM ** -0.5)
        lane = lax.broadcasted_iota(jnp.int32, (TQ, LANES), 1)
        for h in range(NI):
            stash_ref[wexp_at(h)] = jnp.broadcast_to(w_all[:, h:h + 1], (TQ, LANES))
            slab = qi_ref[0, :, (h // 2) * LANES:(h // 2 + 1) * LANES]
            keep = (lane < IDX_HEAD_DIM) if h % 2 == 0 else (lane >= IDX_HEAD_DIM)
            slab_ref[h] = jnp.where(keep, slab, jnp.zeros_like(slab))

        def score_tile(kt, carry):
            s0 = pl.multiple_of(kt * TS, TS)
            kk = kk_ref[0, pl.ds(s0, TS), :]
            acc = jnp.zeros((TQ, TS), F32)
            for h in range(NI):
                lg = lax.dot_general(slab_ref[h], kk, nt_dims, preferred_element_type=F32)
                w = jnp.concatenate([stash_ref[wexp_at(h)]] * reps, axis=1)
                acc = acc + w * jnp.maximum(lg, 0.0)
            sc = jnp.where(admissible(kt), acc, -jnp.inf)
            bits = pltpu.bitcast(sc, jnp.int32)
            key_ref[kt] = bits ^ ((bits >> 31) & jnp.int32(0x7FFFFFFF))
            return carry

        lax.fori_loop(0, n_kt, score_tile, 0)

        def count_keys(pred):
            lane_idx = lax.broadcasted_iota(jnp.int32, (COUNT_ROWS, LANES), 1)
            counts = []
            for rb in range(TQ // COUNT_ROWS):
                rows = pl.ds(rb * COUNT_ROWS, COUNT_ROWS)

                def count_tile(kt, c, rows=rows, rb=rb):
                    keys = key_ref[kt, rows, :]
                    for r in range(reps):
                        s_idx = kt * TS + r * LANES + lane_idx
                        c = c + pred(keys[:, r * LANES:(r + 1) * LANES], s_idx, rb).astype(jnp.int32)
                    return c

                counts.append(_fori_unrolled(n_kt, count_tile, jnp.zeros((COUNT_ROWS, LANES), jnp.int32), 2))
            per_lane = jnp.concatenate(counts, axis=0).astype(F32)
            return jnp.sum(per_lane, axis=1, keepdims=True).astype(jnp.int32)

        def rows_of(x, rb):
            return x[rb * COUNT_ROWS:(rb + 1) * COUNT_ROWS]

        def bit_step(i, carry):
            thr, n_ge = carry
            cand = thr ^ (jnp.int32(1) << (31 - i))
            tot = count_keys(lambda keys, s_idx, rb: keys >= rows_of(cand, rb))
            ok = tot >= topk
            return jnp.where(ok, cand, thr), jnp.where(ok, tot, n_ge)

        thr, n_ge = lax.fori_loop(0, 32, bit_step, (jnp.full((TQ, LANES), INT32_MIN, jnp.int32),
                                                    jnp.full((TQ, 1), n_kt * TS, jnp.int32)))
        thr_w = jnp.concatenate([thr] * reps, axis=1)
        surplus_ties = jnp.max(n_ge) > topk

        def write_bias(select):
            def bias_tile(kt, carry):
                bias = jnp.where(select(key_ref[kt], kt) & admissible(kt), 0.0, -jnp.inf).astype(F32)
                key_ref[kt] = pltpu.bitcast(bias, jnp.int32)
                return carry

            lax.fori_loop(0, n_kt, bias_tile, 0)

        @pl.when(jnp.logical_not(surplus_ties))
        def _():
            write_bias(lambda keys, kt: keys >= thr_w)

        @pl.when(surplus_ties)
        def _():
            n_gt = count_keys(lambda keys, s_idx, rb: keys > rows_of(thr, rb))
            need = topk - n_gt

            def idx_step(i, last):
                cand = last | (jnp.int32(1) << (idx_bits - 1 - i))
                below = count_keys(
                    lambda keys, s_idx, rb: (keys == rows_of(thr, rb)) & (s_idx < rows_of(cand, rb)))
                return jnp.where(below < need, cand, last)

            last = lax.fori_loop(0, idx_bits, idx_step, jnp.zeros((TQ, LANES), jnp.int32))
            last_w = jnp.concatenate([last] * reps, axis=1)

            def select(keys, kt):
                s_idx = kt * TS + lax.broadcasted_iota(jnp.int32, (TQ, TS), 1)
                return (keys > thr_w) | ((keys == thr_w) & (s_idx <= last_w))

            write_bias(select)

    scale = HEAD_DIM ** -0.5
    ones_blk = jnp.ones((TS, HEAD_DIM), BF16)
    for pr in range(GROUP // PAIR):
        heads = range(pr * PAIR, (pr + 1) * PAIR)
        qs = jnp.concatenate([q_ref[0, :, c * HEAD_DIM:(c + 1) * HEAD_DIM] for c in heads], axis=0)
        mp_ref[...] = jnp.full_like(mp_ref, -jnp.inf)

        def logits_tile(kt):
            s0 = pl.multiple_of(kt * TS, TS)
            s = lax.dot_general(qs, k_ref[0, pl.ds(s0, TS), :], nt_dims, preferred_element_type=F32) * scale
            bias = pltpu.bitcast(key_ref[kt], F32)
            s = s + jnp.concatenate([bias] * PAIR, axis=0)
            stash_ref[kt] = s
            mp = mp_ref[...]
            for r in range(reps):
                mp = jnp.maximum(mp, s[:, r * LANES:(r + 1) * LANES])
            mp_ref[...] = mp

        _for_tiles(n_kt, logits_tile)
        m = jnp.max(mp_ref[...], axis=1, keepdims=True)
        m = jnp.where(m == -jnp.inf, 0.0, m)
        m_w = jnp.broadcast_to(m, (PAIR * TQ, TS))
        acc_ref[...] = jnp.zeros_like(acc_ref)

        def pv_tile(kt):
            s0 = pl.multiple_of(kt * TS, TS)
            p = jnp.exp(stash_ref[kt] - m_w).astype(BF16)
            v1 = jnp.concatenate([v_ref[0, pl.ds(s0, TS), :], ones_blk], axis=1)
            acc_ref[...] += jnp.dot(p, v1, preferred_element_type=F32)

        _for_tiles(n_kt, pv_tile)
        acc = acc_ref[...]
        o = acc[:, :HEAD_DIM] / acc[:, HEAD_DIM:]
        for i, c in enumerate(heads):
            o_ref[0, :, c * HEAD_DIM:(c + 1) * HEAD_DIM] = o[i * TQ:(i + 1) * TQ].astype(o_ref.dtype)


def _dsa_attention(qkv, qi, wi, kk, *, n_heads, topk, TQ=256, TS=512):
    B, S, _ = qkv.shape
    n_kv = n_heads // GROUP
    NI = wi.shape[-1]
    TQ = _pick(S, TQ)
    TS = _pick(S, TS)
    n_tiles = S // TS
    assert NI <= n_tiles * PAIR * (TS // LANES), "head-weight table does not fit in the logits stash"
    kern = functools.partial(_dsa_kernel, TQ=TQ, TS=TS, NI=NI, topk=topk, idx_bits=(S - 1).bit_length())
    gw = GROUP * HEAD_DIM
    return pl.pallas_call(
        kern,
        grid=(B, S // TQ, n_kv),
        in_specs=[
            pl.BlockSpec((1, TQ, gw), lambda b, i, g: (b, i, g)),
            pl.BlockSpec((1, S, HEAD_DIM), lambda b, i, g: (b, 0, n_heads + g)),
            pl.BlockSpec((1, S, HEAD_DIM), lambda b, i, g: (b, 0, n_heads + n_kv + g)),
            pl.BlockSpec((1, TQ, NI * IDX_HEAD_DIM), lambda b, i, g: (b, i, 0)),
            pl.BlockSpec((1, TQ, NI), lambda b, i, g: (b, i, 0)),
            pl.BlockSpec((1, S, LANES), lambda b, i, g: (b, 0, 0)),
        ],
        out_specs=pl.BlockSpec((1, TQ, gw), lambda b, i, g: (b, i, g)),
        out_shape=jax.ShapeDtypeStruct((B, S, n_heads * HEAD_DIM), BF16),
        scratch_shapes=[
            pltpu.VMEM((n_tiles, TQ, TS), jnp.int32),
            pltpu.VMEM((n_tiles, PAIR * TQ, TS), F32),
            pltpu.VMEM((NI, TQ, LANES), BF16),
            pltpu.VMEM((PAIR * TQ, LANES), F32),
            pltpu.VMEM((PAIR * TQ, 2 * HEAD_DIM), F32),
        ],
        compiler_params=_compiler_params(("parallel", "parallel", "arbitrary")),
        name="dsa_attention",
    )(qkv, qkv, qkv, qi, wi, kk)


SUB = 16


def _hgrn_kernel(q_ref, f_ref, v_ref, gate_ref, ng_ref, o_ref, state_ref, *, TT, HG):
    @pl.when(pl.program_id(2) == 0)
    def _():
        state_ref[...] = jnp.zeros_like(state_ref)

    C = CHUNK
    row = lax.broadcasted_iota(jnp.int32, (C, C), 0)
    col = lax.broadcasted_iota(jnp.int32, (C, C), 1)
    tril = (row >= col).astype(F32)
    n_sub = C // SUB
    sub_row = lax.broadcasted_iota(jnp.int32, (SUB, C), 0)
    sub_col = lax.broadcasted_iota(jnp.int32, (SUB, C), 1)
    nt_dims = (((1,), (1,)), ((), ()))
    ng = ng_ref[...]

    def chunk_body(c, carry):
        r0 = pl.multiple_of(c * C, C)
        for h in range(HG):
            cols = slice(h * HGRN_EXPAND, (h + 1) * HGRN_EXPAND)
            q = q_ref[0, pl.ds(r0, C), cols]
            f = f_ref[0, pl.ds(r0, C), cols]
            v = v_ref[0, pl.ds(r0, C), cols]
            gate = gate_ref[0, pl.ds(r0, C), cols]
            lf = jnp.log(f)
            kin = 1.0 - f
            b = jnp.dot(tril, lf, preferred_element_type=F32, precision=lax.Precision.HIGHEST)
            v16 = v.astype(BF16)
            st = state_ref[h]
            o = lax.dot_general((q * jnp.exp(b)).astype(BF16), st.astype(BF16), nt_dims,
                                preferred_element_type=F32)
            qparts, kparts = [], []
            for i in range(1, n_sub):
                lo, hi = i * SUB, (i + 1) * SUB
                ref = b[lo - 1:lo, :]
                q_i = q[lo:hi] * jnp.exp(b[lo:hi] - ref)
                k_i = kin[:lo] * jnp.exp(ref - b[:lo])
                tail = [jnp.zeros((C - hi, HGRN_EXPAND), F32)] if hi < C else []
                qparts.append(jnp.concatenate([jnp.zeros((lo, HGRN_EXPAND), F32), q_i] + tail, axis=0))
                kparts.append(jnp.concatenate([k_i, jnp.zeros((C - lo, HGRN_EXPAND), F32)], axis=0))
            qcat = jnp.concatenate(qparts, axis=1).astype(BF16)
            kcat = jnp.concatenate(kparts, axis=1).astype(BF16)
            a_off = lax.dot_general(qcat, kcat, nt_dims, preferred_element_type=F32)
            dparts = []
            for i in range(n_sub):
                blk = slice(i * SUB, (i + 1) * SUB)
                bt, qt, kt_ = b[blk], q[blk], kin[blk]
                a = jnp.zeros((SUB, C), F32)
                for s in range(SUB):
                    e = jnp.exp(bt - bt[s:s + 1, :])
                    a_col = jnp.sum(qt * kt_[s:s + 1, :] * e, axis=-1, keepdims=True)
                    a = jnp.where((sub_col == i * SUB + s) & (sub_row >= s), a_col, a)
                dparts.append(a)
            a_all = a_off + jnp.concatenate(dparts, axis=0)
            o = o + jnp.dot(a_all.astype(BF16), v16, preferred_element_type=F32)
            b_last = b[C - 1:C, :]
            kd = (kin * jnp.exp(b_last - b)).astype(BF16)
            upd = lax.dot_general(v16, kd, (((0,), (0,)), ((), ())), preferred_element_type=F32)
            state_ref[h] = jnp.exp(b_last) * st + upd
            ms = jnp.mean(o * o, axis=-1, keepdims=True)
            o_ref[0, pl.ds(r0, C), cols] = (o * lax.rsqrt(ms + RMS_EPS) * ng * gate).astype(o_ref.dtype)
        return carry

    def two_chunks(i, carry):
        chunk_body(2 * i, carry)
        chunk_body(2 * i + 1, carry)
        return carry

    lax.fori_loop(0, TT // (2 * C), two_chunks, 0)


def _hgrn_scan(q, f, v, gate, norm_g, *, TT=512, HG=8):
    B, S, D = q.shape
    H = D // HGRN_EXPAND
    TT = _pick(S, TT)
    HG = _pick(H, HG)
    blk = pl.BlockSpec((1, TT, HG * HGRN_EXPAND), lambda b, h, t: (b, t, h))
    return pl.pallas_call(
        functools.partial(_hgrn_kernel, TT=TT, HG=HG),
        grid=(B, H // HG, S // TT),
        in_specs=[blk, blk, blk, blk, pl.BlockSpec((1, HGRN_EXPAND), lambda b, h, t: (0, 0))],
        out_specs=blk,
        out_shape=jax.ShapeDtypeStruct((B, S, D), BF16),
        scratch_shapes=[pltpu.VMEM((HG, HGRN_EXPAND, HGRN_EXPAND), F32)],
        compiler_params=_compiler_params(("parallel", "parallel", "arbitrary")),
        name="hgrn_scan",
    )(q, f, v, gate, norm_g.reshape(1, HGRN_EXPAND))


def _rope_tables(seq, dim):
    inv = 1.0 / (ROPE_THETA ** (jnp.arange(0, dim, 2, dtype=F32) / dim))
    ang = jnp.arange(seq, dtype=F32)[:, None] * inv[None, :]
    cos, sin = jnp.cos(ang), jnp.sin(ang)
    reps = LANES // dim
    cos_full = jnp.tile(jnp.concatenate([cos, cos], axis=-1), (1, reps))
    sin_signed = jnp.tile(jnp.concatenate([-sin, sin], axis=-1), (1, reps))
    return cos_full, sin_signed


def _dsa_mixer(xb, B, S, w_in, w_out, kln_g, kln_b):
    D = xb.shape[1]
    n_heads = D // HEAD_DIM
    n_kv = n_heads // GROUP
    d_qkv = (n_heads + 2 * n_kv) * HEAD_DIM
    NI = (w_in[0].shape[-1] - d_qkv - IDX_HEAD_DIM) // (IDX_HEAD_DIM + 1)
    d_qi = NI * IDX_HEAD_DIM
    topk = min(TOPK_MAX, S // 4)
    cos128, sin128 = _rope_tables(S, HEAD_DIM)
    cos64, sin64 = _rope_tables(S, IDX_HEAD_DIM)

    qkv = _matmul(xb, w_in, cols=(0, d_qkv), kind="qkv", out_dtype=BF16, row_aux=(cos128, sin128),
                  static_aux=(n_heads + n_kv,), seq=S)
    qi = _matmul(xb, w_in, cols=(d_qkv, d_qkv + d_qi), kind="rope64", out_dtype=BF16,
                 row_aux=(cos64, sin64), seq=S)
    w_tail_src = w_in[0][w_in[1], :, d_qkv + d_qi:]
    w_wi = w_tail_src[:, :NI]
    w_ki = w_tail_src[:, NI:]
    pad = LANES - IDX_HEAD_DIM - NI
    w_tail = jnp.concatenate([w_ki, w_wi, jnp.zeros((D, pad), BF16)], axis=1)
    g_pad = jnp.pad(kln_g.astype(F32), (0, LANES - IDX_HEAD_DIM)).reshape(1, LANES)
    b_pad = jnp.pad(kln_b.astype(F32), (0, LANES - IDX_HEAD_DIM)).reshape(1, LANES)
    tail = _matmul(xb, w_tail, kind="idx_tail", out_dtype=F32, row_aux=(cos64, sin64),
                   col_aux=(g_pad, b_pad), static_aux=(NI ** -0.5,), seq=S)
    ki = tail[:, :IDX_HEAD_DIM].astype(BF16).reshape(B, S, IDX_HEAD_DIM)
    wi = tail[:, IDX_HEAD_DIM:IDX_HEAD_DIM + NI].reshape(B, S, NI)
    kk = jnp.concatenate([ki, ki], axis=-1)

    o = _dsa_attention(qkv.reshape(B, S, d_qkv), qi.reshape(B, S, d_qi), wi, kk, n_heads=n_heads, topk=topk)
    return _matmul(o.reshape(B * S, D), w_out)


def _hgrn_mixer(xb, B, S, w_in, lb, norm_g, w_out):
    D = xb.shape[1]
    q = _matmul(xb, w_in, cols=(0, D), kind="silu")
    f = _matmul(xb, w_in, cols=(D, 2 * D), kind="fgate", col_aux=(lb.reshape(1, D),))
    v = _matmul(xb, w_in, cols=(2 * D, 3 * D))
    gate = _matmul(xb, w_in, cols=(3 * D, 4 * D), kind="silu")
    shp = (B, S, D)
    o = _hgrn_scan(q.reshape(shp), f.reshape(shp), v.reshape(shp), gate.reshape(shp), norm_g)
    return _matmul(o.reshape(B * S, D), w_out)


def _mlp(xb, w_up, w_down):
    h = _matmul(xb, w_up, kind="relu2", out_dtype=BF16)
    return _matmul(h, w_down)


def kernel(x, attn_w_in, attn_w_out, idx_k_ln_g, idx_k_ln_b, hgrn_w_in, hgrn_lower_bounds, hgrn_norm_g,
           hgrn_w_out, mix_ln_g, mix_ln_b, mlp_w_up, mlp_w_down, mlp_ln_g, mlp_ln_b):
    B, S, D = x.shape
    depth = mix_ln_g.shape[0]
    alpha = (2.0 * depth) ** 0.25
    lb_all = jnp.cumsum(jax.nn.softmax(hgrn_lower_bounds.astype(F32), axis=0), axis=0)
    lb_all = lb_all - lb_all[0:1]
    attn_in16, attn_out16 = attn_w_in.astype(BF16), attn_w_out.astype(BF16)
    hgrn_in16, hgrn_out16 = hgrn_w_in.astype(BF16), hgrn_w_out.astype(BF16)
    up16, down16 = mlp_w_up.astype(BF16), mlp_w_down.astype(BF16)
    xf = x.reshape(B * S, D)
    xb = xf.astype(BF16)
    for layer in range(depth):
        j = layer // 2
        if layer % 2 == 0:
            h = _dsa_mixer(xb, B, S, (attn_in16, j), (attn_out16, j), idx_k_ln_g[j], idx_k_ln_b[j])
        else:
            h = _hgrn_mixer(xb, B, S, (hgrn_in16, j), lb_all[layer], hgrn_norm_g[j], (hgrn_out16, j))
        xf, xb = _residual_ln(xf, h, mix_ln_g[layer], mix_ln_b[layer], alpha)
        h = _mlp(xb, (up16, layer), (down16, layer))
        xf, xb = _residual_ln(xf, h, mlp_ln_g[layer], mlp_ln_b[layer], alpha)
    return xf.reshape(B, S, D)
```
